```python
import math
import jax, jax.numpy as jnp
from jax import lax
import numpy as np

D_MODEL = 1024
BATCH = 1
SEQ = 16384
DEPTH = 2

A_HEADS = 4
A_HEAD_DIM = 64
A_WIDTH = A_HEADS * 2 * A_HEAD_DIM
Q_BLOCK = 128
B_WIDTH = 512
B_GROUPS = 4
B_GROUP_DIM = B_WIDTH // B_GROUPS
B_CHUNK = 128
C_HEADS = 8
C_HEAD_DIM = 64
C_WIDTH = C_HEADS * C_HEAD_DIM
C_PATTERNS = ((128, 1), (512, 4), (2048, 16))
C_BLOCK = 128
N_BRANCH = 3
BRANCH_WIDTH = 512
IN_COLS = 4 * A_WIDTH + 3 * B_WIDTH + 4 * C_WIDTH + N_BRANCH * D_MODEL
ROPE_THETA = 500000.0
ROPE_FRAC = 4
EPS = 1e-6

kernel_name = 'hybrid_diffattn_sgu_dilated_gated_merge'


def rmsnorm(t, g):
    tf = t.astype(jnp.float32)
    y = tf * lax.rsqrt(jnp.mean(tf * tf, axis=-1, keepdims=True) + EPS)
    return (y * g.astype(jnp.float32)).astype(t.dtype)


def layernorm(t, g, b):
    tf = t.astype(jnp.float32)
    mu = jnp.mean(tf, axis=-1, keepdims=True)
    var = jnp.mean(jnp.square(tf - mu), axis=-1, keepdims=True)
    y = (tf - mu) * lax.rsqrt(var + EPS)
    return (y * g.astype(jnp.float32) + b.astype(jnp.float32)).astype(t.dtype)


def partial_rope(t, positions):
    dh = t.shape[-1]
    rot = dh // ROPE_FRAC
    half = rot // 2
    inv = jnp.power(jnp.float32(ROPE_THETA), -jnp.arange(half, dtype=jnp.float32) * 2.0 / rot)
    ang = positions.astype(jnp.float32)[:, :, None] * inv
    cos = jnp.cos(ang)[:, :, None, :]
    sin = jnp.sin(ang)[:, :, None, :]
    tf = t.astype(jnp.float32)
    x1 = tf[..., :half]
    x2 = tf[..., half:rot]
    out = jnp.concatenate([x1 * cos - x2 * sin, x2 * cos + x1 * sin, tf[..., rot:]], axis=-1)
    return out.astype(t.dtype)


def split_cols(proj):
    sizes = [A_WIDTH, A_WIDTH, A_WIDTH, A_WIDTH,
             2 * B_WIDTH, B_WIDTH,
             C_WIDTH, C_WIDTH, C_WIDTH, C_WIDTH,
             N_BRANCH * D_MODEL]
    idx = np.cumsum(sizes)[:-1].tolist()
    return jnp.split(proj, idx, axis=-1)


def diff_attention(q, k, v, lam):
    B, S, H, _, d = q.shape
    nb = S // Q_BLOCK
    scale = 1.0 / math.sqrt(d)
    qb = q.reshape(B, nb, Q_BLOCK, H, 2, d).transpose(1, 0, 2, 3, 4, 5)
    kpos = jnp.arange(S)

    def block(args):
        i, qblk = args
        s = jnp.einsum('bqhmd,bkhmd->bhmqk', qblk, k,
                       preferred_element_type=jnp.float32) * scale
        qpos = i * Q_BLOCK + jnp.arange(Q_BLOCK)
        mask = kpos[None, :] <= qpos[:, None]
        s = jnp.where(mask, s, -jnp.inf)
        p = jax.nn.softmax(s, axis=-1)
        pd = p[:, :, 0] - lam * p[:, :, 1]
        return jnp.einsum('bhqk,bkhe->bqhe', pd.astype(v.dtype), v)

    out = lax.map(block, (jnp.arange(nb), qb))
    return out.transpose(1, 0, 2, 3, 4).reshape(B, S, H, 2 * d)


def dilated_window_attention(q, k, v, window, dil):
    B, S, H, dh = q.shape
    L = S // dil
    n_back = window // dil
    nb = -(-L // C_BLOCK)
    Lp = nb * C_BLOCK
    BD = B * dil
    scale = 1.0 / math.sqrt(dh)

    def to_sub(t):
        t = t.reshape(B, L, dil, H, dh).transpose(0, 2, 1, 3, 4).reshape(BD, L, H, dh)
        return jnp.pad(t, ((0, 0), (0, Lp - L), (0, 0), (0, 0)))

    def windows(t):
        t = jnp.pad(t, ((0, 0), (C_BLOCK, 0), (0, 0), (0, 0))).reshape(BD, nb + 1, C_BLOCK, H, dh)
        return jnp.concatenate([t[:, :-1], t[:, 1:]], axis=2)

    qb = to_sub(q).reshape(BD, nb, C_BLOCK, H, dh)
    kw = windows(to_sub(k))
    vw = windows(to_sub(v))
    s = jnp.einsum('bnqhd,bnkhd->bnhqk', qb, kw,
                   preferred_element_type=jnp.float32) * scale
    qi = jnp.arange(nb)[:, None, None] * C_BLOCK + jnp.arange(C_BLOCK)[None, :, None]
    ki = jnp.arange(nb)[:, None, None] * C_BLOCK - C_BLOCK + jnp.arange(2 * C_BLOCK)[None, None, :]
    rel = qi - ki
    mask = (rel >= 0) & (rel <= n_back) & (ki >= 0)
    s = jnp.where(mask[None, :, None], s, -jnp.inf)
    m = jnp.max(s, axis=-1, keepdims=True)
    e = jnp.exp(s - m)
    den = jnp.sum(e, axis=-1)
    o = jnp.einsum('bnhqk,bnkhd->bnqhd', e, vw.astype(jnp.float32))
    den_q = den.transpose(0, 1, 3, 2)
    o = o / den_q[..., None]
    lse = m[..., 0].transpose(0, 1, 3, 2) + jnp.log(den_q)

    def from_sub(t):
        rest = t.shape[3:]
        t = t.reshape((B, dil, Lp) + rest)[:, :, :L]
        t = jnp.moveaxis(t, 1, 2)
        return t.reshape((B, S) + rest)

    return from_sub(o), from_sub(lse)


def hybrid_layer(x, positions, layer_idx, norm_g, w_in, lam_q1, lam_k1, lam_q2, lam_k2,
                 subln_g, sgu_ln_g, sgu_ln_b, sgu_w, sgu_b, w_branch, w_out):
    B, S, D = x.shape
    h = rmsnorm(x, norm_g)
    proj = jnp.einsum('bsd,dc->bsc', h, w_in)
    aq, ak, av, az, buv, bz, cq, ck, cv, cz, gl = split_cols(proj)

    lam_init = 0.8 - 0.6 * math.exp(-0.3 * layer_idx)
    lam = (jnp.exp(jnp.sum(lam_q1.astype(jnp.float32) * lam_k1.astype(jnp.float32)))
           - jnp.exp(jnp.sum(lam_q2.astype(jnp.float32) * lam_k2.astype(jnp.float32)))
           + lam_init)
    qa = partial_rope(aq.reshape(B, S, 2 * A_HEADS, A_HEAD_DIM), positions).reshape(B, S, A_HEADS, 2, A_HEAD_DIM)
    ka = partial_rope(ak.reshape(B, S, 2 * A_HEADS, A_HEAD_DIM), positions).reshape(B, S, A_HEADS, 2, A_HEAD_DIM)
    va = av.reshape(B, S, A_HEADS, 2 * A_HEAD_DIM)
    oa = diff_attention(qa, ka, va, lam)
    oa = rmsnorm(oa, subln_g) * (1.0 - lam_init)
    ya = oa.reshape(B, S, A_WIDTH) * jax.nn.silu(az)

    uv = jax.nn.gelu(buv)
    u, vb = uv[..., :B_WIDTH], uv[..., B_WIDTH:]
    vb = layernorm(vb, sgu_ln_g, sgu_ln_b).reshape(B, S // B_CHUNK, B_CHUNK, B_GROUPS, B_GROUP_DIM)
    w_causal = sgu_w * jnp.tril(jnp.ones((B_CHUNK, B_CHUNK), dtype=sgu_w.dtype))
    mixed = jnp.einsum('gts,bnsgc->bntgc', w_causal, vb) + sgu_b.T[None, None, :, :, None]
    yb = u * mixed.reshape(B, S, B_WIDTH) * jax.nn.silu(bz)

    qc = partial_rope(cq.reshape(B, S, C_HEADS, C_HEAD_DIM), positions)
    kc = partial_rope(ck.reshape(B, S, C_HEADS, C_HEAD_DIM), positions)
    vc = cv.reshape(B, S, C_HEADS, C_HEAD_DIM)
    outs = []
    lses = []
    for window, dil in C_PATTERNS:
        o_p, lse_p = dilated_window_attention(qc, kc, vc, window, dil)
        outs.append(o_p)
        lses.append(lse_p)
    wts = jax.nn.softmax(jnp.stack(lses, axis=0), axis=0)
    oc = jnp.sum(wts[..., None] * jnp.stack(outs, axis=0), axis=0).astype(x.dtype)
    yc = oc.reshape(B, S, C_WIDTH) * jax.nn.silu(cz)

    ys = jnp.stack([ya, yb, yc], axis=0)
    pb = jnp.einsum('nbsc,ncd->nbsd', ys, w_branch)
    gates = jax.nn.sigmoid(gl.reshape(B, S, N_BRANCH, D))
    merged = jnp.einsum('bsnd,nbsd->bsd', gates, pb)
    return x + jnp.einsum('bsd,de->bse', merged, w_out)


def setup_inputs(seed: int = 0) -> dict:
    key = jax.random.key(seed)
    ks = jax.random.split(key, 16)
    f32 = jnp.float32
    x = jax.random.normal(ks[0], (BATCH, SEQ, D_MODEL), f32)
    positions = jnp.broadcast_to(jnp.arange(SEQ, dtype=jnp.int32)[None, :], (BATCH, SEQ))
    norm_g = 1.0 + 0.05 * jax.random.normal(ks[1], (DEPTH, D_MODEL), f32)
    w_in = jax.random.normal(ks[2], (DEPTH, D_MODEL, IN_COLS), f32) * D_MODEL ** -0.5
    lam_q1 = 0.1 * jax.random.normal(ks[3], (DEPTH, A_HEAD_DIM), f32)
    lam_k1 = 0.1 * jax.random.normal(ks[4], (DEPTH, A_HEAD_DIM), f32)
    lam_q2 = 0.1 * jax.random.normal(ks[5], (DEPTH, A_HEAD_DIM), f32)
    lam_k2 = 0.1 * jax.random.normal(ks[6], (DEPTH, A_HEAD_DIM), f32)
    subln_g = 1.0 + 0.05 * jax.random.normal(ks[7], (DEPTH, 2 * A_HEAD_DIM), f32)
    sgu_ln_g = 1.0 + 0.05 * jax.random.normal(ks[8], (DEPTH, B_WIDTH), f32)
    sgu_ln_b = 0.02 * jax.random.normal(ks[9], (DEPTH, B_WIDTH), f32)
    sgu_w = jax.random.normal(ks[10], (DEPTH, B_GROUPS, B_CHUNK, B_CHUNK), f32) * B_CHUNK ** -0.5
    sgu_b = 1.0 + 0.1 * jax.random.normal(ks[11], (DEPTH, B_GROUPS, B_CHUNK), f32)
    w_branch = jax.random.normal(ks[12], (DEPTH, N_BRANCH, BRANCH_WIDTH, D_MODEL), f32) * BRANCH_WIDTH ** -0.5
    w_out = jax.random.normal(ks[13], (DEPTH, D_MODEL, D_MODEL), f32) * (2.0 * D_MODEL) ** -0.5
    final_g = 1.0 + 0.05 * jax.random.normal(ks[14], (D_MODEL,), f32)
    return {'x': x, 'positions': positions, 'norm_g': norm_g, 'w_in': w_in,
            'lam_q1': lam_q1, 'lam_k1': lam_k1, 'lam_q2': lam_q2, 'lam_k2': lam_k2,
            'subln_g': subln_g, 'sgu_ln_g': sgu_ln_g, 'sgu_ln_b': sgu_ln_b,
            'sgu_w': sgu_w, 'sgu_b': sgu_b, 'w_branch': w_branch, 'w_out': w_out,
            'final_g': final_g}


def reference(x, positions, norm_g, w_in, lam_q1, lam_k1, lam_q2, lam_k2, subln_g,
              sgu_ln_g, sgu_ln_b, sgu_w, sgu_b, w_branch, w_out, final_g):
    h = x
    for l in range(DEPTH):
        h = hybrid_layer(h, positions, l, norm_g[l], w_in[l], lam_q1[l], lam_k1[l],
                         lam_q2[l], lam_k2[l], subln_g[l], sgu_ln_g[l], sgu_ln_b[l],
                         sgu_w[l], sgu_b[l], w_branch[l], w_out[l])
    return rmsnorm(h, final_g)
```

```python
import functools
import math

import jax
import jax.numpy as jnp
from jax import lax
from jax.experimental import pallas as pl
from jax.experimental.pallas import tpu as pltpu

D_MODEL = 1024
A_HEADS = 4
A_HEAD_DIM = 64
A_WIDTH = A_HEADS * 2 * A_HEAD_DIM
B_WIDTH = 512
B_GROUPS = 4
B_GROUP_DIM = B_WIDTH // B_GROUPS
B_CHUNK = 128
C_HEADS = 8
C_HEAD_DIM = 64
C_WIDTH = C_HEADS * C_HEAD_DIM
C_PATTERNS = ((128, 1), (512, 4), (2048, 16))
C_BLOCK = 128
N_BRANCH = 3
BRANCH_WIDTH = 512
ROPE_THETA = 500000.0
ROPE_FRAC = 4
EPS = 1e-6

LANES = 128
VMEM_LIMIT = 56 * 1024 * 1024

_SIZES = (A_WIDTH, A_WIDTH, A_WIDTH, A_WIDTH, 2 * B_WIDTH, B_WIDTH,
          C_WIDTH, C_WIDTH, C_WIDTH, C_WIDTH, N_BRANCH * D_MODEL)
_OFFS = tuple(int(sum(_SIZES[:i])) for i in range(len(_SIZES)))
IN_COLS = int(sum(_SIZES))

F32 = jnp.float32
BF16 = jnp.bfloat16


def _silu(t):
    return t * (1.0 / (1.0 + jnp.exp(-t)))


def _sigmoid(t):
    return 1.0 / (1.0 + jnp.exp(-t))


def _gelu_tanh(t):
    c = math.sqrt(2.0 / math.pi)
    return 0.5 * t * (1.0 + jnp.tanh(c * (t + 0.044715 * (t * t * t))))


def _rope_table_kernel(pos_ref, inv_ref, sgn_ref, cos_ref, sin_ref):
    ang = pos_ref[...].astype(F32) * inv_ref[...]
    cos_ref[...] = jnp.cos(ang)
    sin_ref[...] = jnp.sin(ang) * sgn_ref[...]


def _rope_tables(positions, seq):
    rot = A_HEAD_DIM // ROPE_FRAC
    half = rot // 2
    inv = jnp.power(jnp.float32(ROPE_THETA), -jnp.arange(half, dtype=jnp.float32) * 2.0 / rot)
    zeros = jnp.zeros((A_HEAD_DIM - rot,), F32)
    inv_head = jnp.concatenate([inv, inv, zeros])
    sgn_head = jnp.concatenate([-jnp.ones((half,), F32), jnp.ones((half,), F32), zeros])
    reps = LANES // A_HEAD_DIM
    inv_lane = jnp.tile(inv_head, reps).reshape(1, LANES)
    sgn_lane = jnp.tile(sgn_head, reps).reshape(1, LANES)
    tm = 1024
    return pl.pallas_call(
        _rope_table_kernel,
        grid=(seq // tm,),
        in_specs=[pl.BlockSpec((tm, 1), lambda i: (i, 0)),
                  pl.BlockSpec((1, LANES), lambda i: (0, 0)),
                  pl.BlockSpec((1, LANES), lambda i: (0, 0))],
        out_specs=[pl.BlockSpec((tm, LANES), lambda i: (i, 0)),
                   pl.BlockSpec((tm, LANES), lambda i: (i, 0))],
        out_shape=[jax.ShapeDtypeStruct((seq, LANES), F32)] * 2,
        name="rope_tables",
    )(positions.reshape(seq, 1), inv_lane, sgn_lane)


def _inproj_kernel(x_ref, g_ref, w_ref, cos_ref, sin_ref,
                   qa_ref, ka_ref, va_ref, za_ref, guv_ref, zb_ref,
                   qc_ref, kc_ref, vc_ref, zc_ref, gate_ref):
    x = x_ref[...]
    h = x * lax.rsqrt(jnp.mean(x * x, axis=-1, keepdims=True) + EPS) * g_ref[...]
    h = h.astype(BF16)
    cos = cos_ref[...]
    sin = sin_ref[...]
    tm = x.shape[0]
    lane = lax.broadcasted_iota(jnp.int32, (tm, LANES), 1)
    first_half = (lane % A_HEAD_DIM) < (A_HEAD_DIM // ROPE_FRAC // 2)
    shift = A_HEAD_DIM // ROPE_FRAC // 2

    def proj(lo, n):
        return jnp.dot(h, w_ref[:, lo:lo + n], preferred_element_type=F32)

    def rope_store(seg, out_ref, scale):
        for c in range(out_ref.shape[1] // LANES):
            t = proj(_OFFS[seg] + c * LANES, LANES)
            partner = jnp.where(first_half, pltpu.roll(t, LANES - shift, 1), pltpu.roll(t, shift, 1))
            r = t * cos + partner * sin
            if scale != 1.0:
                r = r * scale
            out_ref[:, c * LANES:(c + 1) * LANES] = r.astype(out_ref.dtype)

    def act_store(seg, out_ref, fn, width=512):
        n = out_ref.shape[1]
        for c in range(n // width):
            t = proj(_OFFS[seg] + c * width, width)
            out_ref[:, c * width:(c + 1) * width] = fn(t).astype(out_ref.dtype)

    ident = lambda t: t
    rope_store(0, qa_ref, 1.0 / math.sqrt(A_HEAD_DIM))
    rope_store(1, ka_ref, 1.0)
    act_store(2, va_ref, ident)
    act_store(3, za_ref, _silu)
    act_store(4, guv_ref, _gelu_tanh)
    act_store(5, zb_ref, _silu)
    rope_store(6, qc_ref, 1.0 / math.sqrt(C_HEAD_DIM))
    rope_store(7, kc_ref, 1.0)
    act_store(8, vc_ref, ident)
    act_store(9, zc_ref, _silu)
    act_store(10, gate_ref, _sigmoid)


def _inproj(x2, norm_g, w_bf16, cos_t, sin_t, tm=256):
    seq = x2.shape[0]
    row = lambda i: (i, 0)
    fixed = lambda i: (0, 0)
    widths = (A_WIDTH, A_WIDTH, A_WIDTH, A_WIDTH, 2 * B_WIDTH, B_WIDTH,
              C_WIDTH, C_WIDTH, C_WIDTH, C_WIDTH, N_BRANCH * D_MODEL)
    dtypes = (BF16, BF16, BF16, BF16, BF16, BF16, F32, F32, F32, BF16, BF16)
    return pl.pallas_call(
        _inproj_kernel,
        grid=(seq // tm,),
        in_specs=[pl.BlockSpec((tm, D_MODEL), row),
                  pl.BlockSpec((1, D_MODEL), fixed),
                  pl.BlockSpec((D_MODEL, IN_COLS), fixed, pipeline_mode=pl.Buffered(1)),
                  pl.BlockSpec((tm, LANES), row),
                  pl.BlockSpec((tm, LANES), row)],
        out_specs=[pl.BlockSpec((tm, w), row) for w in widths],
        out_shape=[jax.ShapeDtypeStruct((seq, w), d) for w, d in zip(widths, dtypes)],
        compiler_params=pltpu.CompilerParams(dimension_semantics=("arbitrary",),
                                             vmem_limit_bytes=VMEM_LIMIT),
        name="inproj",
    )(x2, norm_g.reshape(1, D_MODEL), w_bf16, cos_t, sin_t)


def _attn_a_kernel(q_ref, k_ref, v_ref, za_ref, g_ref, lq1_ref, lk1_ref, lq2_ref, lk2_ref,
                   o_ref, m_scr, l_scr, acc_scr, *, tq, lam_init):
    i = pl.program_id(1)
    d = A_HEAD_DIM
    nt = (((1,), (1,)), ((), ()))
    q = q_ref[...]
    m_scr[...] = jnp.full(m_scr.shape, -jnp.inf, F32)
    l_scr[...] = jnp.zeros(l_scr.shape, F32)
    acc_scr[...] = jnp.zeros(acc_scr.shape, F32)

    def step(j, masked):
        start = pl.multiple_of(j * tq, tq)
        vb = v_ref[pl.ds(start, tq), :]
        for m in range(2):
            kb = k_ref[pl.ds(start, tq), m * d:(m + 1) * d]
            s = lax.dot_general(q[:, m * d:(m + 1) * d], kb, nt, preferred_element_type=F32)
            if masked:
                rowi = lax.broadcasted_iota(jnp.int32, (tq, tq), 0)
                coli = lax.broadcasted_iota(jnp.int32, (tq, tq), 1)
                s = jnp.where(coli <= rowi, s, -jnp.inf)
            m_old = m_scr[m]
            m_new = jnp.maximum(m_old, jnp.max(s, axis=-1, keepdims=True))
            p = jnp.exp(s - m_new)
            alpha = jnp.exp(m_old - m_new)
            l_scr[m] = alpha * l_scr[m] + jnp.sum(p, axis=-1, keepdims=True)
            acc_scr[m] = alpha * acc_scr[m] + jnp.dot(p.astype(BF16), vb, preferred_element_type=F32)
            m_scr[m] = m_new

    def body(j, carry):
        step(j, False)
        return carry

    lax.fori_loop(0, i, body, 0)
    step(i, True)

    lam = (jnp.exp(jnp.sum(lq1_ref[...] * lk1_ref[...], axis=-1, keepdims=True))
           - jnp.exp(jnp.sum(lq2_ref[...] * lk2_ref[...], axis=-1, keepdims=True)) + lam_init)
    o = acc_scr[0] / l_scr[0] - lam * (acc_scr[1] / l_scr[1])
    y = o * lax.rsqrt(jnp.mean(o * o, axis=-1, keepdims=True) + EPS) * g_ref[...]
    y = y * (1.0 - lam_init)
    o_ref[...] = (y * za_ref[...].astype(F32)).astype(o_ref.dtype)


def _attn_a(qa, ka, va, za, subln_g, lq1, lk1, lq2, lk2, lam_init, tq=512):
    seq = qa.shape[0]
    blk = lambda h, i: (i, h)
    whole = lambda h, i: (0, h)
    fixed = lambda h, i: (0, 0)
    vec = lambda a: a.reshape(1, -1)
    return pl.pallas_call(
        functools.partial(_attn_a_kernel, tq=tq, lam_init=lam_init),
        grid=(A_HEADS, seq // tq),
        in_specs=[pl.BlockSpec((tq, LANES), blk),
                  pl.BlockSpec((seq, LANES), whole),
                  pl.BlockSpec((seq, LANES), whole),
                  pl.BlockSpec((tq, LANES), blk),
                  pl.BlockSpec((1, 2 * A_HEAD_DIM), fixed),
                  pl.BlockSpec((1, A_HEAD_DIM), fixed),
                  pl.BlockSpec((1, A_HEAD_DIM), fixed),
                  pl.BlockSpec((1, A_HEAD_DIM), fixed),
                  pl.BlockSpec((1, A_HEAD_DIM), fixed)],
        out_specs=pl.BlockSpec((tq, LANES), blk),
        out_shape=jax.ShapeDtypeStruct((seq, A_WIDTH), BF16),
        scratch_shapes=[pltpu.VMEM((2, tq, 1), F32),
                        pltpu.VMEM((2, tq, 1), F32),
                        pltpu.VMEM((2, tq, 2 * A_HEAD_DIM), F32)],
        compiler_params=pltpu.CompilerParams(dimension_semantics=("arbitrary", "arbitrary"),
                                             vmem_limit_bytes=VMEM_LIMIT),
        name="attn_a",
    )(qa, ka, va, za, vec(subln_g), vec(lq1), vec(lk1), vec(lq2), vec(lk2))


C_TILE = C_BLOCK * max(dil for _, dil in C_PATTERNS)


def _attn_c_kernel(q_ref, kp_ref, kc_ref, vp_ref, vc_ref, z_ref, o_ref,
                   kk_scr, vv_scr, out_scr, lse_scr):
    t_idx = pl.program_id(1)
    dh = C_HEAD_DIM
    nt = (((1,), (1,)), ((), ()))
    kk_scr[0:C_TILE, :] = kp_ref[...]
    kk_scr[C_TILE:2 * C_TILE, :] = kc_ref[...]
    vv_scr[0:C_TILE, :] = vp_ref[...]
    vv_scr[C_TILE:2 * C_TILE, :] = vc_ref[...]

    qi = lax.broadcasted_iota(jnp.int32, (C_BLOCK, 2 * C_BLOCK), 0)
    ki = lax.broadcasted_iota(jnp.int32, (C_BLOCK, 2 * C_BLOCK), 1)
    has_prev_tile = t_idx > 0

    for p, (window, dil) in enumerate(C_PATTERNS):
        n_back = window // dil
        rel = qi + C_BLOCK - ki
        band = (rel >= 0) & (rel <= n_back)
        span = C_BLOCK * dil

        def block(b, carry, p=p, dil=dil, band=band, span=span):
            u = b // dil
            r = b % dil
            q_start = u * span + r
            k_start = C_TILE + (u - 1) * span + r
            if dil == 1:
                q2 = q_ref[pl.ds(q_start, C_BLOCK), :]
                k2 = kk_scr[pl.ds(k_start, 2 * C_BLOCK), :]
                v2 = vv_scr[pl.ds(k_start, 2 * C_BLOCK), :]
            else:
                q2 = q_ref[pl.ds(q_start, C_BLOCK, stride=dil), :]
                k2 = kk_scr[pl.ds(k_start, 2 * C_BLOCK, stride=dil), :]
                v2 = vv_scr[pl.ds(k_start, 2 * C_BLOCK, stride=dil), :]
            q2 = q2.astype(BF16)
            k2 = k2.astype(BF16)
            v2 = v2.astype(BF16)
            valid = band & ((ki >= C_BLOCK) | (u > 0) | has_prev_tile)
            outs = []
            lses = []
            for hh in range(LANES // dh):
                sl = slice(hh * dh, (hh + 1) * dh)
                s = lax.dot_general(q2[:, sl], k2[:, sl], nt, preferred_element_type=F32)
                s = jnp.where(valid, s, -jnp.inf)
                mx = jnp.max(s, axis=-1, keepdims=True)
                e = jnp.exp(s - mx)
                den = jnp.sum(e, axis=-1, keepdims=True)
                o = jnp.dot(e.astype(BF16), v2[:, sl], preferred_element_type=F32) / den
                outs.append(o)
                lses.append(jnp.broadcast_to(mx + jnp.log(den), (C_BLOCK, dh)))
            o2 = jnp.concatenate(outs, axis=-1)
            l2 = jnp.concatenate(lses, axis=-1)
            if dil == 1:
                out_scr[p, pl.ds(q_start, C_BLOCK), :] = o2
                lse_scr[p, pl.ds(q_start, C_BLOCK), :] = l2
            else:
                out_scr[p, pl.ds(q_start, C_BLOCK, stride=dil), :] = o2
                lse_scr[p, pl.ds(q_start, C_BLOCK, stride=dil), :] = l2
            return carry

        lax.fori_loop(0, C_TILE // C_BLOCK, block, 0)

    l0, l1, l2 = lse_scr[0], lse_scr[1], lse_scr[2]
    mx = jnp.maximum(jnp.maximum(l0, l1), l2)
    w0 = jnp.exp(l0 - mx)
    w1 = jnp.exp(l1 - mx)
    w2 = jnp.exp(l2 - mx)
    oc = (w0 * out_scr[0] + w1 * out_scr[1] + w2 * out_scr[2]) / (w0 + w1 + w2)
    o_ref[...] = (oc * z_ref[...].astype(F32)).astype(o_ref.dtype)


def _attn_c(qc, kc, vc, zc):
    seq = qc.shape[0]
    cur = lambda c, t: (t, c)
    prev = lambda c, t: (jnp.maximum(t - 1, 0), c)
    tile = (C_TILE, LANES)
    return pl.pallas_call(
        _attn_c_kernel,
        grid=(C_WIDTH // LANES, seq // C_TILE),
        in_specs=[pl.BlockSpec(tile, cur),
                  pl.BlockSpec(tile, prev), pl.BlockSpec(tile, cur),
                  pl.BlockSpec(tile, prev), pl.BlockSpec(tile, cur),
                  pl.BlockSpec(tile, cur)],
        out_specs=pl.BlockSpec(tile, cur),
        out_shape=jax.ShapeDtypeStruct((seq, C_WIDTH), BF16),
        scratch_shapes=[pltpu.VMEM((2 * C_TILE, LANES), F32),
                        pltpu.VMEM((2 * C_TILE, LANES), F32),
                        pltpu.VMEM((len(C_PATTERNS), C_TILE, LANES), F32),
                        pltpu.VMEM((len(C_PATTERNS), C_TILE, LANES), F32)],
        compiler_params=pltpu.CompilerParams(dimension_semantics=("arbitrary", "arbitrary"),
                                             vmem_limit_bytes=VMEM_LIMIT),
        name="attn_c",
    )(qc, kc, kc, vc, vc, zc)


def _merge_kernel(x_ref, ya_ref, yc_ref, guv_ref, zb_ref, gate_ref, lng_ref, lnb_ref,
                  sw_ref, sb_ref, wb_ref, wo_ref, fg_ref, o_ref, *, final_norm):
    tm = x_ref.shape[0]
    guv = guv_ref[...]
    u = guv[:, :B_WIDTH].astype(F32)
    vb = guv[:, B_WIDTH:].astype(F32)
    mu = jnp.mean(vb, axis=-1, keepdims=True)
    var = jnp.mean(jnp.square(vb - mu), axis=-1, keepdims=True)
    vb = ((vb - mu) * lax.rsqrt(var + EPS) * lng_ref[...] + lnb_ref[...]).astype(BF16)
    ti = lax.broadcasted_iota(jnp.int32, (B_CHUNK, B_CHUNK), 0)
    si = lax.broadcasted_iota(jnp.int32, (B_CHUNK, B_CHUNK), 1)
    causal = si <= ti
    rows = []
    for c in range(tm // B_CHUNK):
        cols = []
        for g in range(B_GROUPS):
            w = jnp.where(causal, sw_ref[g], 0.0).astype(BF16)
            blk = vb[c * B_CHUNK:(c + 1) * B_CHUNK, g * B_GROUP_DIM:(g + 1) * B_GROUP_DIM]
            mixed = jnp.dot(w, blk, preferred_element_type=F32) + sb_ref[:, g:g + 1]
            cols.append(mixed)
        rows.append(jnp.concatenate(cols, axis=-1))
    mixed = jnp.concatenate(rows, axis=0)
    yb = (u * mixed * zb_ref[...].astype(F32)).astype(BF16)

    gates = gate_ref[...]
    merged = jnp.zeros((tm, D_MODEL), F32)
    for n, y in enumerate((ya_ref[...], yb, yc_ref[...])):
        pb = jnp.dot(y, wb_ref[n], preferred_element_type=F32)
        merged = merged + gates[:, n * D_MODEL:(n + 1) * D_MODEL].astype(F32) * pb
    out = x_ref[...] + jnp.dot(merged.astype(BF16), wo_ref[...], preferred_element_type=F32)
    if final_norm:
        out = out * lax.rsqrt(jnp.mean(out * out, axis=-1, keepdims=True) + EPS) * fg_ref[...]
    o_ref[...] = out


def _merge(x2, ya, yc, guv, zb, gates, ln_g, ln_b, sgu_w, sgu_b, wb_bf16, wo_bf16, final_g,
           final_norm, tm=256):
    seq = x2.shape[0]
    row = lambda i: (i, 0)
    fixed2 = lambda i: (0, 0)
    fixed3 = lambda i: (0, 0, 0)
    return pl.pallas_call(
        functools.partial(_merge_kernel, final_norm=final_norm),
        grid=(seq // tm,),
        in_specs=[pl.BlockSpec((tm, D_MODEL), row),
                  pl.BlockSpec((tm, A_WIDTH), row),
                  pl.BlockSpec((tm, C_WIDTH), row),
                  pl.BlockSpec((tm, 2 * B_WIDTH), row),
                  pl.BlockSpec((tm, B_WIDTH), row),
                  pl.BlockSpec((tm, N_BRANCH * D_MODEL), row),
                  pl.BlockSpec((1, B_WIDTH), fixed2),
                  pl.BlockSpec((1, B_WIDTH), fixed2),
                  pl.BlockSpec((B_GROUPS, B_CHUNK, B_CHUNK), fixed3),
                  pl.BlockSpec((B_CHUNK, B_GROUPS), fixed2),
                  pl.BlockSpec((N_BRANCH, BRANCH_WIDTH, D_MODEL), fixed3),
                  pl.BlockSpec((D_MODEL, D_MODEL), fixed2),
                  pl.BlockSpec((1, D_MODEL), fixed2)],
        out_specs=pl.BlockSpec((tm, D_MODEL), row),
        out_shape=jax.ShapeDtypeStruct((seq, D_MODEL), F32),
        compiler_params=pltpu.CompilerParams(dimension_semantics=("arbitrary",),
                                             vmem_limit_bytes=VMEM_LIMIT),
        name="merge",
    )(x2, ya, yc, guv, zb, gates, ln_g.reshape(1, -1), ln_b.reshape(1, -1), sgu_w, sgu_b.T,
      wb_bf16, wo_bf16, final_g.reshape(1, -1))


def kernel(x, positions, norm_g, w_in, lam_q1, lam_k1, lam_q2, lam_k2, subln_g, sgu_ln_g, sgu_ln_b,
           sgu_w, sgu_b, w_branch, w_out, final_g):
    batch, seq, _ = x.shape
    depth = norm_g.shape[0]
    assert batch == 1 and seq % C_TILE == 0
    h = x.reshape(seq, D_MODEL)
    cos_t, sin_t = _rope_tables(positions.reshape(seq), seq)
    for l in range(depth):
        lam_init = 0.8 - 0.6 * math.exp(-0.3 * l)
        (qa, ka, va, za, guv, zb, qc, kc, vc, zc, gates) = _inproj(
            h, norm_g[l], w_in[l].astype(BF16), cos_t, sin_t)
        ya = _attn_a(qa, ka, va, za, subln_g[l], lam_q1[l], lam_k1[l], lam_q2[l], lam_k2[l], lam_init)
        yc = _attn_c(qc, kc, vc, zc)
        h = _merge(h, ya, yc, guv, zb, gates, sgu_ln_g[l], sgu_ln_b[l], sgu_w[l], sgu_b[l],
                   w_branch[l].astype(BF16), w_out[l].astype(BF16), final_g,
                   final_norm=(l == depth - 1))
    return h.reshape(batch, seq, D_MODEL)
```

```python
import functools
import math

import jax
import jax.numpy as jnp
from jax import lax
from jax.experimental import pallas as pl
from jax.experimental.pallas import tpu as pltpu

D_MODEL = 1024
A_HEADS = 4
A_HEAD_DIM = 64
A_WIDTH = A_HEADS * 2 * A_HEAD_DIM
B_WIDTH = 512
B_GROUPS = 4
B_GROUP_DIM = B_WIDTH // B_GROUPS
B_CHUNK = 128
C_HEADS = 8
C_HEAD_DIM = 64
C_WIDTH = C_HEADS * C_HEAD_DIM
C_PATTERNS = ((128, 1), (512, 4), (2048, 16))
C_BLOCK = 128
N_BRANCH = 3
BRANCH_WIDTH = 512
ROPE_THETA = 500000.0
ROPE_FRAC = 4
EPS = 1e-6

LANES = 128
VMEM_LIMIT = 56 * 1024 * 1024

_SIZES = (A_WIDTH, A_WIDTH, A_WIDTH, A_WIDTH, 2 * B_WIDTH, B_WIDTH,
          C_WIDTH, C_WIDTH, C_WIDTH, C_WIDTH, N_BRANCH * D_MODEL)
_OFFS = tuple(int(sum(_SIZES[:i])) for i in range(len(_SIZES)))
IN_COLS = int(sum(_SIZES))

F32 = jnp.float32
BF16 = jnp.bfloat16


def _silu(t):
    return t * (1.0 / (1.0 + jnp.exp(-t)))


def _sigmoid(t):
    return 1.0 / (1.0 + jnp.exp(-t))


def _gelu_tanh(t):
    c = math.sqrt(2.0 / math.pi)
    return 0.5 * t * (1.0 + jnp.tanh(c * (t + 0.044715 * (t * t * t))))


def _rope_table_kernel(pos_ref, inv_ref, sgn_ref, cos_ref, sin_ref):
    ang = pos_ref[...].astype(F32) * inv_ref[...]
    cos_ref[...] = jnp.cos(ang)
    sin_ref[...] = jnp.sin(ang) * sgn_ref[...]


def _rope_tables(positions, seq):
    rot = A_HEAD_DIM // ROPE_FRAC
    half = rot // 2
    inv = jnp.power(jnp.float32(ROPE_THETA), -jnp.arange(half, dtype=jnp.float32) * 2.0 / rot)
    zeros = jnp.zeros((A_HEAD_DIM - rot,), F32)
    inv_head = jnp.concatenate([inv, inv, zeros])
    sgn_head = jnp.concatenate([-jnp.ones((half,), F32), jnp.ones((half,), F32), zeros])
    reps = LANES // A_HEAD_DIM
    inv_lane = jnp.tile(inv_head, reps).reshape(1, LANES)
    sgn_lane = jnp.tile(sgn_head, reps).reshape(1, LANES)
    tm = 1024
    return pl.pallas_call(
        _rope_table_kernel,
        grid=(seq // tm,),
        in_specs=[pl.BlockSpec((tm, 1), lambda i: (i, 0)),
                  pl.BlockSpec((1, LANES), lambda i: (0, 0)),
                  pl.BlockSpec((1, LANES), lambda i: (0, 0))],
        out_specs=[pl.BlockSpec((tm, LANES), lambda i: (i, 0)),
                   pl.BlockSpec((tm, LANES), lambda i: (i, 0))],
        out_shape=[jax.ShapeDtypeStruct((seq, LANES), F32)] * 2,
        name="rope_tables",
    )(positions.reshape(seq, 1), inv_lane, sgn_lane)


def _inproj_kernel(x_ref, g_ref, w_ref, cos_ref, sin_ref,
                   qa_ref, ka_ref, va_ref, za_ref, guv_ref, zb_ref,
                   qc_ref, kc_ref, vc_ref, zc_ref, gate_ref):
    x = x_ref[...]
    h = x * lax.rsqrt(jnp.mean(x * x, axis=-1, keepdims=True) + EPS) * g_ref[...]
    h = h.astype(BF16)
    cos = cos_ref[...]
    sin = sin_ref[...]
    tm = x.shape[0]
    lane = lax.broadcasted_iota(jnp.int32, (tm, LANES), 1)
    first_half = (lane % A_HEAD_DIM) < (A_HEAD_DIM // ROPE_FRAC // 2)
    shift = A_HEAD_DIM // ROPE_FRAC // 2

    def proj(lo, n):
        return jnp.dot(h, w_ref[:, lo:lo + n], preferred_element_type=F32)

    def rope_store(seg, out_ref, scale):
        for c in range(out_ref.shape[1] // LANES):
            t = proj(_OFFS[seg] + c * LANES, LANES)
            partner = jnp.where(first_half, pltpu.roll(t, LANES - shift, 1), pltpu.roll(t, shift, 1))
            r = t * cos + partner * sin
            if scale != 1.0:
                r = r * scale
            out_ref[:, c * LANES:(c + 1) * LANES] = r.astype(out_ref.dtype)

    def act_store(seg, out_ref, fn, width=512):
        n = out_ref.shape[1]
        for c in range(n // width):
            t = proj(_OFFS[seg] + c * width, width)
            out_ref[:, c * width:(c + 1) * width] = fn(t).astype(out_ref.dtype)

    ident = lambda t: t
    rope_store(0, qa_ref, LOG2E / math.sqrt(A_HEAD_DIM))
    rope_store(1, ka_ref, 1.0)
    act_store(2, va_ref, ident)
    act_store(3, za_ref, _silu)
    act_store(4, guv_ref, _gelu_tanh)
    act_store(5, zb_ref, _silu)
    rope_store(6, qc_ref, 1.0 / math.sqrt(C_HEAD_DIM))
    rope_store(7, kc_ref, 1.0)
    act_store(8, vc_ref, ident)
    act_store(9, zc_ref, _silu)
    act_store(10, gate_ref, _sigmoid)


def _inproj(x2, norm_g, w_bf16, cos_t, sin_t, tm=256):
    seq = x2.shape[0]
    row = lambda i: (i, 0)
    fixed = lambda i: (0, 0)
    widths = (A_WIDTH, A_WIDTH, A_WIDTH, A_WIDTH, 2 * B_WIDTH, B_WIDTH,
              C_WIDTH, C_WIDTH, C_WIDTH, C_WIDTH, N_BRANCH * D_MODEL)
    dtypes = (BF16, BF16, BF16, BF16, BF16, BF16, F32, F32, F32, BF16, BF16)
    return pl.pallas_call(
        _inproj_kernel,
        grid=(seq // tm,),
        in_specs=[pl.BlockSpec((tm, D_MODEL), row),
                  pl.BlockSpec((1, D_MODEL), fixed),
                  pl.BlockSpec((D_MODEL, IN_COLS), fixed, pipeline_mode=pl.Buffered(1)),
                  pl.BlockSpec((tm, LANES), row),
                  pl.BlockSpec((tm, LANES), row)],
        out_specs=[pl.BlockSpec((tm, w), row) for w in widths],
        out_shape=[jax.ShapeDtypeStruct((seq, w), d) for w, d in zip(widths, dtypes)],
        compiler_params=pltpu.CompilerParams(dimension_semantics=("arbitrary",),
                                             vmem_limit_bytes=VMEM_LIMIT),
        name="inproj",
    )(x2, norm_g.reshape(1, D_MODEL), w_bf16, cos_t, sin_t)


LOG2E = math.log2(math.e)


def _attn_a_kernel(qt_ref, k_ref, vt_ref, za_ref, g_ref, lq1_ref, lk1_ref, lq2_ref, lk2_ref,
                   o_ref, qz_scr, m_scr, l_scr, acc_scr, s_a, s_b, p_a, p_b, al_a, al_b,
                   *, tq, tk, lam_init):
    assert tq == tk
    i = pl.program_id(1)
    d = A_HEAD_DIM
    qt = qt_ref[...]
    row = lax.broadcasted_iota(jnp.int32, qt.shape, 0)
    zero = jnp.zeros_like(qt)
    qz_scr[0] = jnp.where(row < d, qt, zero)
    qz_scr[1] = jnp.where(row >= d, qt, zero)
    m_scr[...] = jnp.full(m_scr.shape, -jnp.inf, F32)
    l_scr[...] = jnp.zeros(l_scr.shape, F32)
    acc_scr[...] = jnp.zeros(acc_scr.shape, F32)

    def scores(j, s_buf):
        kb = k_ref[pl.ds(pl.multiple_of(j * tk, tk), tk), :]
        for m in range(2):
            s_buf[m] = jnp.dot(kb, qz_scr[m], preferred_element_type=F32)

    def softmax(s_buf, p_buf, al_buf, masked):
        for m in range(2):
            s = s_buf[m]
            if masked:
                keyi = lax.broadcasted_iota(jnp.int32, (tk, tq), 0)
                qryi = lax.broadcasted_iota(jnp.int32, (tk, tq), 1)
                s = jnp.where(keyi <= qryi, s, -jnp.inf)
            m_old = m_scr[m]
            m_new = jnp.maximum(m_old, jnp.max(s, axis=0, keepdims=True))
            p = jnp.exp2(s - m_new)
            alpha = jnp.exp2(m_old - m_new)
            l_scr[m] = alpha * l_scr[m] + jnp.sum(p, axis=0, keepdims=True)
            m_scr[m] = m_new
            al_buf[m] = alpha
            p_buf[m] = p.astype(BF16)

    def values(j, p_buf, al_buf):
        vtb = vt_ref[j]
        for m in range(2):
            acc_scr[m] = al_buf[m] * acc_scr[m] + jnp.dot(vtb, p_buf[m], preferred_element_type=F32)

    scores(0, s_a)

    @pl.when(i >= 1)
    def _():
        scores(1, s_b)
        softmax(s_a, p_a, al_a, False)

    def pair(u, carry):
        t = 2 + 2 * u
        values(t - 2, p_a, al_a)
        scores(t, s_a)
        softmax(s_b, p_b, al_b, False)
        scores(t + 1, s_b)
        values(t - 1, p_b, al_b)
        softmax(s_a, p_a, al_a, False)
        return carry

    lax.fori_loop(0, jnp.maximum(i - 1, 0) // 2, pair, 0)

    @pl.when((i >= 2) & (i % 2 == 0))
    def _():
        values(i - 2, p_a, al_a)
        scores(i, s_a)
        softmax(s_b, p_b, al_b, False)

    @pl.when(i % 2 == 0)
    def _():
        @pl.when(i >= 1)
        def _():
            values(i - 1, p_b, al_b)
        softmax(s_a, p_a, al_a, True)
        values(i, p_a, al_a)

    @pl.when(i % 2 == 1)
    def _():
        values(i - 1, p_a, al_a)
        softmax(s_b, p_b, al_b, True)
        values(i, p_b, al_b)

    lam = (jnp.exp(jnp.sum(lq1_ref[...] * lk1_ref[...], axis=-1, keepdims=True))
           - jnp.exp(jnp.sum(lq2_ref[...] * lk2_ref[...], axis=-1, keepdims=True)) + lam_init)
    o = acc_scr[0] * (1.0 / l_scr[0]) - lam * (acc_scr[1] * (1.0 / l_scr[1]))
    y = o * lax.rsqrt(jnp.mean(o * o, axis=0, keepdims=True) + EPS) * g_ref[...]
    y = (y * (1.0 - lam_init)).T
    o_ref[...] = (y * za_ref[...].astype(F32)).astype(o_ref.dtype)


def _attn_a(qa, ka, va, za, subln_g, lq1, lk1, lq2, lk2, lam_init, tq=512, tk=512):
    seq = qa.shape[0]
    nk = seq // tk
    qt = qa.T
    vt = va.reshape(nk, tk, A_HEADS, 2 * A_HEAD_DIM).transpose(2, 0, 3, 1)
    fixed = lambda h, i: (0, 0)
    vec = lambda a: a.reshape(1, -1)
    return pl.pallas_call(
        functools.partial(_attn_a_kernel, tq=tq, tk=tk, lam_init=lam_init),
        grid=(A_HEADS, seq // tq),
        in_specs=[pl.BlockSpec((LANES, tq), lambda h, i: (h, i)),
                  pl.BlockSpec((seq, LANES), lambda h, i: (0, h)),
                  pl.BlockSpec((None, nk, LANES, tk), lambda h, i: (h, 0, 0, 0)),
                  pl.BlockSpec((tq, LANES), lambda h, i: (i, h)),
                  pl.BlockSpec((2 * A_HEAD_DIM, 1), fixed),
                  pl.BlockSpec((1, A_HEAD_DIM), fixed),
                  pl.BlockSpec((1, A_HEAD_DIM), fixed),
                  pl.BlockSpec((1, A_HEAD_DIM), fixed),
                  pl.BlockSpec((1, A_HEAD_DIM), fixed)],
        out_specs=pl.BlockSpec((tq, LANES), lambda h, i: (i, h)),
        out_shape=jax.ShapeDtypeStruct((seq, A_WIDTH), BF16),
        scratch_shapes=[pltpu.VMEM((2, LANES, tq), BF16),
                        pltpu.VMEM((2, 1, tq), F32),
                        pltpu.VMEM((2, 1, tq), F32),
                        pltpu.VMEM((2, 2 * A_HEAD_DIM, tq), F32),
                        pltpu.VMEM((2, tk, tq), F32), pltpu.VMEM((2, tk, tq), F32),
                        pltpu.VMEM((2, tk, tq), BF16), pltpu.VMEM((2, tk, tq), BF16),
                        pltpu.VMEM((2, 1, tq), F32), pltpu.VMEM((2, 1, tq), F32)],
        compiler_params=pltpu.CompilerParams(dimension_semantics=("arbitrary", "arbitrary"),
                                             vmem_limit_bytes=VMEM_LIMIT),
        name="attn_a",
    )(qt, ka, vt, za, subln_g.reshape(-1, 1), vec(lq1), vec(lk1), vec(lq2), vec(lk2))


C_TILE = C_BLOCK * max(dil for _, dil in C_PATTERNS)


def _attn_c_kernel(q_ref, kp_ref, kc_ref, vp_ref, vc_ref, z_ref, o_ref,
                   kk_scr, vv_scr, out_scr, lse_scr):
    t_idx = pl.program_id(1)
    dh = C_HEAD_DIM
    nt = (((1,), (1,)), ((), ()))
    kk_scr[0:C_TILE, :] = kp_ref[...]
    kk_scr[C_TILE:2 * C_TILE, :] = kc_ref[...]
    vv_scr[0:C_TILE, :] = vp_ref[...]
    vv_scr[C_TILE:2 * C_TILE, :] = vc_ref[...]

    qi = lax.broadcasted_iota(jnp.int32, (C_BLOCK, 2 * C_BLOCK), 0)
    ki = lax.broadcasted_iota(jnp.int32, (C_BLOCK, 2 * C_BLOCK), 1)
    has_prev_tile = t_idx > 0

    for p, (window, dil) in enumerate(C_PATTERNS):
        n_back = window // dil
        rel = qi + C_BLOCK - ki
        band = (rel >= 0) & (rel <= n_back)
        span = C_BLOCK * dil

        def block(b, carry, p=p, dil=dil, band=band, span=span):
            u = b // dil
            r = b % dil
            q_start = u * span + r
            k_start = C_TILE + (u - 1) * span + r
            if dil == 1:
                q2 = q_ref[pl.ds(q_start, C_BLOCK), :]
                k2 = kk_scr[pl.ds(k_start, 2 * C_BLOCK), :]
                v2 = vv_scr[pl.ds(k_start, 2 * C_BLOCK), :]
            else:
                q2 = q_ref[pl.ds(q_start, C_BLOCK, stride=dil), :]
                k2 = kk_scr[pl.ds(k_start, 2 * C_BLOCK, stride=dil), :]
                v2 = vv_scr[pl.ds(k_start, 2 * C_BLOCK, stride=dil), :]
            q2 = q2.astype(BF16)
            k2 = k2.astype(BF16)
            v2 = v2.astype(BF16)
            valid = band & ((ki >= C_BLOCK) | (u > 0) | has_prev_tile)
            outs = []
            lses = []
            for hh in range(LANES // dh):
                sl = slice(hh * dh, (hh + 1) * dh)
                s = lax.dot_general(q2[:, sl], k2[:, sl], nt, preferred_element_type=F32)
                s = jnp.where(valid, s, -jnp.inf)
                mx = jnp.max(s, axis=-1, keepdims=True)
                e = jnp.exp(s - mx)
                den = jnp.sum(e, axis=-1, keepdims=True)
                o = jnp.dot(e.astype(BF16), v2[:, sl], preferred_element_type=F32) / den
                outs.append(o)
                lses.append(jnp.broadcast_to(mx + jnp.log(den), (C_BLOCK, dh)))
            o2 = jnp.concatenate(outs, axis=-1)
            l2 = jnp.concatenate(lses, axis=-1)
            if dil == 1:
                out_scr[p, pl.ds(q_start, C_BLOCK), :] = o2
                lse_scr[p, pl.ds(q_start, C_BLOCK), :] = l2
            else:
                out_scr[p, pl.ds(q_start, C_BLOCK, stride=dil), :] = o2
                lse_scr[p, pl.ds(q_start, C_BLOCK, stride=dil), :] = l2
            return carry

        lax.fori_loop(0, C_TILE // C_BLOCK, block, 0)

    l0, l1, l2 = lse_scr[0], lse_scr[1], lse_scr[2]
    mx = jnp.maximum(jnp.maximum(l0, l1), l2)
    w0 = jnp.exp(l0 - mx)
    w1 = jnp.exp(l1 - mx)
    w2 = jnp.exp(l2 - mx)
    oc = (w0 * out_scr[0] + w1 * out_scr[1] + w2 * out_scr[2]) / (w0 + w1 + w2)
    o_ref[...] = (oc * z_ref[...].astype(F32)).astype(o_ref.dtype)


def _attn_c(qc, kc, vc, zc):
    seq = qc.shape[0]
    cur = lambda c, t: (t, c)
    prev = lambda c, t: (jnp.maximum(t - 1, 0), c)
    tile = (C_TILE, LANES)
    return pl.pallas_call(
        _attn_c_kernel,
        grid=(C_WIDTH // LANES, seq // C_TILE),
        in_specs=[pl.BlockSpec(tile, cur),
                  pl.BlockSpec(tile, prev), pl.BlockSpec(tile, cur),
                  pl.BlockSpec(tile, prev), pl.BlockSpec(tile, cur),
                  pl.BlockSpec(tile, cur)],
        out_specs=pl.BlockSpec(tile, cur),
        out_shape=jax.ShapeDtypeStruct((seq, C_WIDTH), BF16),
        scratch_shapes=[pltpu.VMEM((2 * C_TILE, LANES), F32),
                        pltpu.VMEM((2 * C_TILE, LANES), F32),
                        pltpu.VMEM((len(C_PATTERNS), C_TILE, LANES), F32),
                        pltpu.VMEM((len(C_PATTERNS), C_TILE, LANES), F32)],
        compiler_params=pltpu.CompilerParams(dimension_semantics=("arbitrary", "arbitrary"),
                                             vmem_limit_bytes=VMEM_LIMIT),
        name="attn_c",
    )(qc, kc, kc, vc, vc, zc)


def _merge_kernel(x_ref, ya_ref, yc_ref, guv_ref, zb_ref, gate_ref, lng_ref, lnb_ref,
                  sw_ref, sb_ref, wb_ref, wo_ref, fg_ref, o_ref, *, final_norm):
    tm = x_ref.shape[0]
    guv = guv_ref[...]
    u = guv[:, :B_WIDTH].astype(F32)
    vb = guv[:, B_WIDTH:].astype(F32)
    mu = jnp.mean(vb, axis=-1, keepdims=True)
    var = jnp.mean(jnp.square(vb - mu), axis=-1, keepdims=True)
    vb = ((vb - mu) * lax.rsqrt(var + EPS) * lng_ref[...] + lnb_ref[...]).astype(BF16)
    ti = lax.broadcasted_iota(jnp.int32, (B_CHUNK, B_CHUNK), 0)
    si = lax.broadcasted_iota(jnp.int32, (B_CHUNK, B_CHUNK), 1)
    causal = si <= ti
    rows = []
    for c in range(tm // B_CHUNK):
        cols = []
        for g in range(B_GROUPS):
            w = jnp.where(causal, sw_ref[g], 0.0).astype(BF16)
            blk = vb[c * B_CHUNK:(c + 1) * B_CHUNK, g * B_GROUP_DIM:(g + 1) * B_GROUP_DIM]
            mixed = jnp.dot(w, blk, preferred_element_type=F32) + sb_ref[:, g:g + 1]
            cols.append(mixed)
        rows.append(jnp.concatenate(cols, axis=-1))
    mixed = jnp.concatenate(rows, axis=0)
    yb = (u * mixed * zb_ref[...].astype(F32)).astype(BF16)

    gates = gate_ref[...]
    merged = jnp.zeros((tm, D_MODEL), F32)
    for n, y in enumerate((ya_ref[...], yb, yc_ref[...])):
        pb = jnp.dot(y, wb_ref[n], preferred_element_type=F32)
        merged = merged + gates[:, n * D_MODEL:(n + 1) * D_MODEL].astype(F32) * pb
    out = x_ref[...] + jnp.dot(merged.astype(BF16), wo_ref[...], preferred_element_type=F32)
    if final_norm:
        out = out * lax.rsqrt(jnp.mean(out * out, axis=-1, keepdims=True) + EPS) * fg_ref[...]
    o_ref[...] = out


def _merge(x2, ya, yc, guv, zb, gates, ln_g, ln_b, sgu_w, sgu_b, wb_bf16, wo_bf16, final_g,
           final_norm, tm=256):
    seq = x2.shape[0]
    row = lambda i: (i, 0)
    fixed2 = lambda i: (0, 0)
    fixed3 = lambda i: (0, 0, 0)
    return pl.pallas_call(
        functools.partial(_merge_kernel, final_norm=final_norm),
        grid=(seq // tm,),
        in_specs=[pl.BlockSpec((tm, D_MODEL), row),
                  pl.BlockSpec((tm, A_WIDTH), row),
                  pl.BlockSpec((tm, C_WIDTH), row),
                  pl.BlockSpec((tm, 2 * B_WIDTH), row),
                  pl.BlockSpec((tm, B_WIDTH), row),
                  pl.BlockSpec((tm, N_BRANCH * D_MODEL), row),
                  pl.BlockSpec((1, B_WIDTH), fixed2),
                  pl.BlockSpec((1, B_WIDTH), fixed2),
                  pl.BlockSpec((B_GROUPS, B_CHUNK, B_CHUNK), fixed3),
                  pl.BlockSpec((B_CHUNK, B_GROUPS), fixed2),
                  pl.BlockSpec((N_BRANCH, BRANCH_WIDTH, D_MODEL), fixed3),
                  pl.BlockSpec((D_MODEL, D_MODEL), fixed2),
                  pl.BlockSpec((1, D_MODEL), fixed2)],
        out_specs=pl.BlockSpec((tm, D_MODEL), row),
        out_shape=jax.ShapeDtypeStruct((seq, D_MODEL), F32),
        compiler_params=pltpu.CompilerParams(dimension_semantics=("arbitrary",),
                                             vmem_limit_bytes=VMEM_LIMIT),
        name="merge",
    )(x2, ya, yc, guv, zb, gates, ln_g.reshape(1, -1), ln_b.reshape(1, -1), sgu_w, sgu_b.T,
      wb_bf16, wo_bf16, final_g.reshape(1, -1))


def kernel(x, positions, norm_g, w_in, lam_q1, lam_k1, lam_q2, lam_k2, subln_g, sgu_ln_g, sgu_ln_b,
           sgu_w, sgu_b, w_branch, w_out, final_g):
    batch, seq, _ = x.shape
    depth = norm_g.shape[0]
    assert batch == 1 and seq % C_TILE == 0
    h = x.reshape(seq, D_MODEL)
    cos_t, sin_t = _rope_tables(positions.reshape(seq), seq)
    for l in range(depth):
        lam_init = 0.8 - 0.6 * math.exp(-0.3 * l)
        (qa, ka, va, za, guv, zb, qc, kc, vc, zc, gates) = _inproj(
            h, norm_g[l], w_in[l].astype(BF16), cos_t, sin_t)
        ya = _attn_a(qa, ka, va, za, subln_g[l], lam_q1[l], lam_k1[l], lam_q2[l], lam_k2[l], lam_init)
        yc = _attn_c(qc, kc, vc, zc)
        h = _merge(h, ya, yc, guv, zb, gates, sgu_ln_g[l], sgu_ln_b[l], sgu_w[l], sgu_b[l],
                   w_branch[l].astype(BF16), w_out[l].astype(BF16), final_g,
                   final_norm=(l == depth - 1))
    return h.reshape(batch, seq, D_MODEL)
```

```python
import functools
import math

import jax
import jax.numpy as jnp
from jax import lax
from jax.experimental import pallas as pl
from jax.experimental.pallas import tpu as pltpu

D_MODEL = 1024
A_HEADS = 4
A_HEAD_DIM = 64
A_WIDTH = A_HEADS * 2 * A_HEAD_DIM
B_WIDTH = 512
B_GROUPS = 4
B_GROUP_DIM = B_WIDTH // B_GROUPS
B_CHUNK = 128
C_HEADS = 8
C_HEAD_DIM = 64
C_WIDTH = C_HEADS * C_HEAD_DIM
C_PATTERNS = ((128, 1), (512, 4), (2048, 16))
C_BLOCK = 128
N_BRANCH = 3
BRANCH_WIDTH = 512
ROPE_THETA = 500000.0
ROPE_FRAC = 4
EPS = 1e-6

LANES = 128
VMEM_LIMIT = 56 * 1024 * 1024

_SIZES = (A_WIDTH, A_WIDTH, A_WIDTH, A_WIDTH, 2 * B_WIDTH, B_WIDTH,
          C_WIDTH, C_WIDTH, C_WIDTH, C_WIDTH, N_BRANCH * D_MODEL)
_OFFS = tuple(int(sum(_SIZES[:i])) for i in range(len(_SIZES)))
IN_COLS = int(sum(_SIZES))

F32 = jnp.float32
BF16 = jnp.bfloat16


def _silu(t):
    return t * (1.0 / (1.0 + jnp.exp(-t)))


def _sigmoid(t):
    return 1.0 / (1.0 + jnp.exp(-t))


def _gelu_tanh(t):
    c = math.sqrt(2.0 / math.pi)
    return 0.5 * t * (1.0 + jnp.tanh(c * (t + 0.044715 * (t * t * t))))


def _rope_table_kernel(pos_ref, inv_ref, sgn_ref, cos_ref, sin_ref):
    ang = pos_ref[...].astype(F32) * inv_ref[...]
    cos_ref[...] = jnp.cos(ang)
    sin_ref[...] = jnp.sin(ang) * sgn_ref[...]


def _rope_tables(positions, seq):
    rot = A_HEAD_DIM // ROPE_FRAC
    half = rot // 2
    inv = jnp.power(jnp.float32(ROPE_THETA), -jnp.arange(half, dtype=jnp.float32) * 2.0 / rot)
    zeros = jnp.zeros((A_HEAD_DIM - rot,), F32)
    inv_head = jnp.concatenate([inv, inv, zeros])
    sgn_head = jnp.concatenate([-jnp.ones((half,), F32), jnp.ones((half,), F32), zeros])
    reps = LANES // A_HEAD_DIM
    inv_lane = jnp.tile(inv_head, reps).reshape(1, LANES)
    sgn_lane = jnp.tile(sgn_head, reps).reshape(1, LANES)
    tm = 1024
    return pl.pallas_call(
        _rope_table_kernel,
        grid=(seq // tm,),
        in_specs=[pl.BlockSpec((tm, 1), lambda i: (i, 0)),
                  pl.BlockSpec((1, LANES), lambda i: (0, 0)),
                  pl.BlockSpec((1, LANES), lambda i: (0, 0))],
        out_specs=[pl.BlockSpec((tm, LANES), lambda i: (i, 0)),
                   pl.BlockSpec((tm, LANES), lambda i: (i, 0))],
        out_shape=[jax.ShapeDtypeStruct((seq, LANES), F32)] * 2,
        name="rope_tables",
    )(positions.reshape(seq, 1), inv_lane, sgn_lane)


def _inproj_kernel(x_ref, g_ref, w_ref, cos_ref, sin_ref,
                   qa_ref, ka_ref, va_ref, za_ref, guv_ref, zb_ref,
                   qc_ref, kc_ref, vc_ref, zc_ref, gate_ref):
    x = x_ref[...]
    h = x * lax.rsqrt(jnp.mean(x * x, axis=-1, keepdims=True) + EPS) * g_ref[...]
    h = h.astype(BF16)
    cos = cos_ref[...]
    sin = sin_ref[...]
    tm = x.shape[0]
    lane = lax.broadcasted_iota(jnp.int32, (tm, LANES), 1)
    first_half = (lane % A_HEAD_DIM) < (A_HEAD_DIM // ROPE_FRAC // 2)
    shift = A_HEAD_DIM // ROPE_FRAC // 2

    def proj(lo, n):
        return jnp.dot(h, w_ref[:, lo:lo + n], preferred_element_type=F32)

    def rope_store(seg, out_ref, scale):
        for c in range(out_ref.shape[1] // LANES):
            t = proj(_OFFS[seg] + c * LANES, LANES)
            partner = jnp.where(first_half, pltpu.roll(t, LANES - shift, 1), pltpu.roll(t, shift, 1))
            r = t * cos + partner * sin
            if scale != 1.0:
                r = r * scale
            out_ref[:, c * LANES:(c + 1) * LANES] = r.astype(out_ref.dtype)

    def act_store(seg, out_ref, fn, width=512):
        n = out_ref.shape[1]
        for c in range(n // width):
            t = proj(_OFFS[seg] + c * width, width)
            out_ref[:, c * width:(c + 1) * width] = fn(t).astype(out_ref.dtype)

    ident = lambda t: t
    rope_store(0, qa_ref, LOG2E / math.sqrt(A_HEAD_DIM))
    rope_store(1, ka_ref, 1.0)
    act_store(2, va_ref, ident)
    act_store(3, za_ref, _silu)
    act_store(4, guv_ref, _gelu_tanh)
    act_store(5, zb_ref, _silu)
    rope_store(6, qc_ref, 1.0 / math.sqrt(C_HEAD_DIM))
    rope_store(7, kc_ref, 1.0)
    act_store(8, vc_ref, ident)
    act_store(9, zc_ref, _silu)
    act_store(10, gate_ref, _sigmoid)


def _inproj(x2, norm_g, w_bf16, cos_t, sin_t, tm=256):
    seq = x2.shape[0]
    row = lambda i: (i, 0)
    fixed = lambda i: (0, 0)
    widths = (A_WIDTH, A_WIDTH, A_WIDTH, A_WIDTH, 2 * B_WIDTH, B_WIDTH,
              C_WIDTH, C_WIDTH, C_WIDTH, C_WIDTH, N_BRANCH * D_MODEL)
    dtypes = (BF16, BF16, BF16, BF16, BF16, BF16, F32, F32, F32, BF16, BF16)
    return pl.pallas_call(
        _inproj_kernel,
        grid=(seq // tm,),
        in_specs=[pl.BlockSpec((tm, D_MODEL), row),
                  pl.BlockSpec((1, D_MODEL), fixed),
                  pl.BlockSpec((D_MODEL, IN_COLS), fixed, pipeline_mode=pl.Buffered(1)),
                  pl.BlockSpec((tm, LANES), row),
                  pl.BlockSpec((tm, LANES), row)],
        out_specs=[pl.BlockSpec((tm, w), row) for w in widths],
        out_shape=[jax.ShapeDtypeStruct((seq, w), d) for w, d in zip(widths, dtypes)],
        compiler_params=pltpu.CompilerParams(dimension_semantics=("arbitrary",),
                                             vmem_limit_bytes=VMEM_LIMIT),
        name="inproj",
    )(x2, norm_g.reshape(1, D_MODEL), w_bf16, cos_t, sin_t)


LOG2E = math.log2(math.e)
A_ONES_ROWS = 16
A_QCHUNK = 256


def _attn_a_kernel(qt_ref, k_ref, vt_ref, za_ref, g_ref, lq1_ref, lk1_ref, lq2_ref, lk2_ref,
                   o_ref, qz_scr, m_scr, acc_scr, s_a, s_b, p_a, p_b, al_a, al_b,
                   *, tq, tk, lam_init):
    assert tq == tk
    i = pl.program_id(1)
    d = A_HEAD_DIM
    qt = qt_ref[...]
    row = lax.broadcasted_iota(jnp.int32, qt.shape, 0)
    zero = jnp.zeros_like(qt)
    qz_scr[0] = jnp.where(row < d, qt, zero)
    qz_scr[1] = jnp.where(row >= d, qt, zero)
    m_scr[...] = jnp.full(m_scr.shape, -jnp.inf, F32)
    acc_scr[...] = jnp.zeros(acc_scr.shape, F32)

    chunks = [(m, slice(h * A_QCHUNK, (h + 1) * A_QCHUNK)) for h in range(tq // A_QCHUNK) for m in range(2)]

    def scores(j, s_buf, c):
        m, qs = c
        kb = k_ref[pl.ds(pl.multiple_of(j * tk, tk), tk), :]
        s_buf[m, :, qs] = jnp.dot(kb, qz_scr[m, :, qs], preferred_element_type=F32)

    def softmax(s_buf, p_buf, al_buf, masked, c):
        m, qs = c
        s = s_buf[m, :, qs]
        if masked:
            keyi = lax.broadcasted_iota(jnp.int32, s.shape, 0)
            qryi = lax.broadcasted_iota(jnp.int32, s.shape, 1) + qs.start
            s = jnp.where(keyi <= qryi, s, -jnp.inf)
        m_old = m_scr[m, :, qs]
        m_new = jnp.maximum(m_old, jnp.max(s, axis=0, keepdims=True))
        p_buf[m, :, qs] = jnp.exp2((s - m_new).astype(BF16))
        al_buf[m, :, qs] = jnp.exp2(m_old - m_new)
        m_scr[m, :, qs] = m_new

    def values(j, p_buf, al_buf, c):
        m, qs = c
        vtb = vt_ref[j]
        acc_scr[m, :, qs] = (al_buf[m, :, qs] * acc_scr[m, :, qs]
                             + jnp.dot(vtb, p_buf[m, :, qs], preferred_element_type=F32))

    def stage(val=None, sco=None, sm=None):
        for c in chunks:
            if val is not None:
                values(*val, c)
            if sco is not None:
                scores(*sco, c)
            if sm is not None:
                softmax(*sm, c)

    A = (p_a, al_a)
    B = (p_b, al_b)
    stage(sco=(0, s_a))

    @pl.when(i >= 1)
    def _():
        stage(sco=(1, s_b), sm=(s_a, *A, False))

    def pair(u, carry):
        t = 2 + 2 * u
        stage(val=(t - 2, *A), sco=(t, s_a), sm=(s_b, *B, False))
        stage(val=(t - 1, *B), sco=(t + 1, s_b), sm=(s_a, *A, False))
        return carry

    lax.fori_loop(0, jnp.maximum(i - 1, 0) // 2, pair, 0)

    @pl.when((i >= 2) & (i % 2 == 0))
    def _():
        stage(val=(i - 2, *A), sco=(i, s_a), sm=(s_b, *B, False))

    @pl.when(i % 2 == 0)
    def _():
        @pl.when(i >= 1)
        def _():
            stage(val=(i - 1, *B))
        stage(sm=(s_a, *A, True))
        stage(val=(i, *A))

    @pl.when(i % 2 == 1)
    def _():
        stage(val=(i - 1, *A), sm=(s_b, *B, True))
        stage(val=(i, *B))

    lam = (jnp.exp(jnp.sum(lq1_ref[...] * lk1_ref[...], axis=-1, keepdims=True))
           - jnp.exp(jnp.sum(lq2_ref[...] * lk2_ref[...], axis=-1, keepdims=True)) + lam_init)
    dv = 2 * A_HEAD_DIM
    o0 = acc_scr[0, 0:dv, :] * (1.0 / acc_scr[0, dv:dv + 1, :])
    o1 = acc_scr[1, 0:dv, :] * (1.0 / acc_scr[1, dv:dv + 1, :])
    o = o0 - lam * o1
    y = o * lax.rsqrt(jnp.mean(o * o, axis=0, keepdims=True) + EPS) * g_ref[...]
    y = (y * (1.0 - lam_init)).T
    o_ref[...] = (y * za_ref[...].astype(F32)).astype(o_ref.dtype)


def _attn_a(qa, ka, va, za, subln_g, lq1, lk1, lq2, lk2, lam_init, tq=512, tk=512):
    seq = qa.shape[0]
    nk = seq // tk
    qt = qa.T
    vt = va.reshape(nk, tk, A_HEADS, 2 * A_HEAD_DIM).transpose(2, 0, 3, 1)
    vt = jnp.concatenate([vt, jnp.ones((A_HEADS, nk, A_ONES_ROWS, tk), BF16)], axis=2)
    fixed = lambda h, i: (0, 0)
    vec = lambda a: a.reshape(1, -1)
    return pl.pallas_call(
        functools.partial(_attn_a_kernel, tq=tq, tk=tk, lam_init=lam_init),
        grid=(A_HEADS, seq // tq),
        in_specs=[pl.BlockSpec((LANES, tq), lambda h, i: (h, i)),
                  pl.BlockSpec((seq, LANES), lambda h, i: (0, h)),
                  pl.BlockSpec((None, nk, LANES + A_ONES_ROWS, tk), lambda h, i: (h, 0, 0, 0)),
                  pl.BlockSpec((tq, LANES), lambda h, i: (i, h)),
                  pl.BlockSpec((2 * A_HEAD_DIM, 1), fixed),
                  pl.BlockSpec((1, A_HEAD_DIM), fixed),
                  pl.BlockSpec((1, A_HEAD_DIM), fixed),
                  pl.BlockSpec((1, A_HEAD_DIM), fixed),
                  pl.BlockSpec((1, A_HEAD_DIM), fixed)],
        out_specs=pl.BlockSpec((tq, LANES), lambda h, i: (i, h)),
        out_shape=jax.ShapeDtypeStruct((seq, A_WIDTH), BF16),
        scratch_shapes=[pltpu.VMEM((2, LANES, tq), BF16),
                        pltpu.VMEM((2, 1, tq), F32),
                        pltpu.VMEM((2, 2 * A_HEAD_DIM + A_ONES_ROWS, tq), F32),
                        pltpu.VMEM((2, tk, tq), F32), pltpu.VMEM((2, tk, tq), F32),
                        pltpu.VMEM((2, tk, tq), BF16), pltpu.VMEM((2, tk, tq), BF16),
                        pltpu.VMEM((2, 1, tq), F32), pltpu.VMEM((2, 1, tq), F32)],
        compiler_params=pltpu.CompilerParams(dimension_semantics=("arbitrary", "arbitrary"),
                                             vmem_limit_bytes=VMEM_LIMIT),
        name="attn_a",
    )(qt, ka, vt, za, subln_g.reshape(-1, 1), vec(lq1), vec(lk1), vec(lq2), vec(lk2))


C_TILE = C_BLOCK * max(dil for _, dil in C_PATTERNS)
C_UNROLL = 4


def _attn_c_kernel(q_ref, kp_ref, kc_ref, vp_ref, vc_ref, z_ref, o_ref,
                   kk_scr, vv_scr, out_scr, lse_scr):
    t_idx = pl.program_id(1)
    dh = C_HEAD_DIM
    nt = (((1,), (1,)), ((), ()))
    kk_scr[0:C_TILE, :] = kp_ref[...]
    kk_scr[C_TILE:2 * C_TILE, :] = kc_ref[...]
    vv_scr[0:C_TILE, :] = vp_ref[...]
    vv_scr[C_TILE:2 * C_TILE, :] = vc_ref[...]

    qi = lax.broadcasted_iota(jnp.int32, (C_BLOCK, 2 * C_BLOCK), 0)
    ki = lax.broadcasted_iota(jnp.int32, (C_BLOCK, 2 * C_BLOCK), 1)
    has_prev_tile = t_idx > 0

    for p, (window, dil) in enumerate(C_PATTERNS):
        n_back = window // dil
        rel = qi + C_BLOCK - ki
        band = (rel >= 0) & (rel <= n_back)
        span = C_BLOCK * dil

        def block(b, p=p, dil=dil, band=band, span=span):
            u = b // dil
            r = b % dil
            q_start = u * span + r
            k_start = C_TILE + (u - 1) * span + r
            if dil == 1:
                q2 = q_ref[pl.ds(q_start, C_BLOCK), :]
                k2 = kk_scr[pl.ds(k_start, 2 * C_BLOCK), :]
                v2 = vv_scr[pl.ds(k_start, 2 * C_BLOCK), :]
            else:
                q2 = q_ref[pl.ds(q_start, C_BLOCK, stride=dil), :]
                k2 = kk_scr[pl.ds(k_start, 2 * C_BLOCK, stride=dil), :]
                v2 = vv_scr[pl.ds(k_start, 2 * C_BLOCK, stride=dil), :]
            q2 = q2.astype(BF16)
            k2 = k2.astype(BF16)
            v2 = v2.astype(BF16)
            valid = band & ((ki >= C_BLOCK) | (u > 0) | has_prev_tile)
            outs = []
            lses = []
            for hh in range(LANES // dh):
                sl = slice(hh * dh, (hh + 1) * dh)
                s = lax.dot_general(q2[:, sl], k2[:, sl], nt, preferred_element_type=F32)
                s = jnp.where(valid, s, -jnp.inf)
                mx = jnp.max(s, axis=-1, keepdims=True)
                e = jnp.exp(s - mx)
                den = jnp.sum(e, axis=-1, keepdims=True)
                o = jnp.dot(e.astype(BF16), v2[:, sl], preferred_element_type=F32) / den
                outs.append(o)
                lses.append(jnp.broadcast_to(mx + jnp.log(den), (C_BLOCK, dh)))
            o2 = jnp.concatenate(outs, axis=-1)
            l2 = jnp.concatenate(lses, axis=-1)
            if dil == 1:
                out_scr[p, pl.ds(q_start, C_BLOCK), :] = o2
                lse_scr[p, pl.ds(q_start, C_BLOCK), :] = l2
            else:
                out_scr[p, pl.ds(q_start, C_BLOCK, stride=dil), :] = o2
                lse_scr[p, pl.ds(q_start, C_BLOCK, stride=dil), :] = l2

        def blocks(bb, carry, block=block):
            for k in range(C_UNROLL):
                block(bb * C_UNROLL + k)
            return carry

        lax.fori_loop(0, C_TILE // C_BLOCK // C_UNROLL, blocks, 0)

    l0, l1, l2 = lse_scr[0], lse_scr[1], lse_scr[2]
    mx = jnp.maximum(jnp.maximum(l0, l1), l2)
    w0 = jnp.exp(l0 - mx)
    w1 = jnp.exp(l1 - mx)
    w2 = jnp.exp(l2 - mx)
    oc = (w0 * out_scr[0] + w1 * out_scr[1] + w2 * out_scr[2]) / (w0 + w1 + w2)
    o_ref[...] = (oc * z_ref[...].astype(F32)).astype(o_ref.dtype)


def _attn_c(qc, kc, vc, zc):
    seq = qc.shape[0]
    cur = lambda c, t: (t, c)
    prev = lambda c, t: (jnp.maximum(t - 1, 0), c)
    tile = (C_TILE, LANES)
    return pl.pallas_call(
        _attn_c_kernel,
        grid=(C_WIDTH // LANES, seq // C_TILE),
        in_specs=[pl.BlockSpec(tile, cur),
                  pl.BlockSpec(tile, prev), pl.BlockSpec(tile, cur),
                  pl.BlockSpec(tile, prev), pl.BlockSpec(tile, cur),
                  pl.BlockSpec(tile, cur)],
        out_specs=pl.BlockSpec(tile, cur),
        out_shape=jax.ShapeDtypeStruct((seq, C_WIDTH), BF16),
        scratch_shapes=[pltpu.VMEM((2 * C_TILE, LANES), F32),
                        pltpu.VMEM((2 * C_TILE, LANES), F32),
                        pltpu.VMEM((len(C_PATTERNS), C_TILE, LANES), F32),
                        pltpu.VMEM((len(C_PATTERNS), C_TILE, LANES), F32)],
        compiler_params=pltpu.CompilerParams(dimension_semantics=("arbitrary", "arbitrary"),
                                             vmem_limit_bytes=VMEM_LIMIT),
        name="attn_c",
    )(qc, kc, kc, vc, vc, zc)


def _merge_kernel(x_ref, ya_ref, yc_ref, guv_ref, zb_ref, gate_ref, lng_ref, lnb_ref,
                  sw_ref, sb_ref, wb_ref, wo_ref, fg_ref, o_ref, *, final_norm):
    tm = x_ref.shape[0]
    guv = guv_ref[...]
    u = guv[:, :B_WIDTH].astype(F32)
    vb = guv[:, B_WIDTH:].astype(F32)
    mu = jnp.mean(vb, axis=-1, keepdims=True)
    var = jnp.mean(jnp.square(vb - mu), axis=-1, keepdims=True)
    vb = ((vb - mu) * lax.rsqrt(var + EPS) * lng_ref[...] + lnb_ref[...]).astype(BF16)
    ti = lax.broadcasted_iota(jnp.int32, (B_CHUNK, B_CHUNK), 0)
    si = lax.broadcasted_iota(jnp.int32, (B_CHUNK, B_CHUNK), 1)
    causal = si <= ti
    rows = []
    for c in range(tm // B_CHUNK):
        cols = []
        for g in range(B_GROUPS):
            w = jnp.where(causal, sw_ref[g], 0.0).astype(BF16)
            blk = vb[c * B_CHUNK:(c + 1) * B_CHUNK, g * B_GROUP_DIM:(g + 1) * B_GROUP_DIM]
            mixed = jnp.dot(w, blk, preferred_element_type=F32) + sb_ref[:, g:g + 1]
            cols.append(mixed)
        rows.append(jnp.concatenate(cols, axis=-1))
    mixed = jnp.concatenate(rows, axis=0)
    yb = (u * mixed * zb_ref[...].astype(F32)).astype(BF16)

    gates = gate_ref[...]
    merged = jnp.zeros((tm, D_MODEL), F32)
    for n, y in enumerate((ya_ref[...], yb, yc_ref[...])):
        pb = jnp.dot(y, wb_ref[n], preferred_element_type=F32)
        merged = merged + gates[:, n * D_MODEL:(n + 1) * D_MODEL].astype(F32) * pb
    out = x_ref[...] + jnp.dot(merged.astype(BF16), wo_ref[...], preferred_element_type=F32)
    if final_norm:
        out = out * lax.rsqrt(jnp.mean(out * out, axis=-1, keepdims=True) + EPS) * fg_ref[...]
    o_ref[...] = out


def _merge(x2, ya, yc, guv, zb, gates, ln_g, ln_b, sgu_w, sgu_b, wb_bf16, wo_bf16, final_g,
           final_norm, tm=256):
    seq = x2.shape[0]
    row = lambda i: (i, 0)
    fixed2 = lambda i: (0, 0)
    fixed3 = lambda i: (0, 0, 0)
    return pl.pallas_call(
        functools.partial(_merge_kernel, final_norm=final_norm),
        grid=(seq // tm,),
        in_specs=[pl.BlockSpec((tm, D_MODEL), row),
                  pl.BlockSpec((tm, A_WIDTH), row),
                  pl.BlockSpec((tm, C_WIDTH), row),
                  pl.BlockSpec((tm, 2 * B_WIDTH), row),
                  pl.BlockSpec((tm, B_WIDTH), row),
                  pl.BlockSpec((tm, N_BRANCH * D_MODEL), row),
                  pl.BlockSpec((1, B_WIDTH), fixed2),
                  pl.BlockSpec((1, B_WIDTH), fixed2),
                  pl.BlockSpec((B_GROUPS, B_CHUNK, B_CHUNK), fixed3),
                  pl.BlockSpec((B_CHUNK, B_GROUPS), fixed2),
                  pl.BlockSpec((N_BRANCH, BRANCH_WIDTH, D_MODEL), fixed3),
                  pl.BlockSpec((D_MODEL, D_MODEL), fixed2),
                  pl.BlockSpec((1, D_MODEL), fixed2)],
        out_specs=pl.BlockSpec((tm, D_MODEL), row),
        out_shape=jax.ShapeDtypeStruct((seq, D_MODEL), F32),
        compiler_params=pltpu.CompilerParams(dimension_semantics=("arbitrary",),
                                             vmem_limit_bytes=VMEM_LIMIT),
        name="merge",
    )(x2, ya, yc, guv, zb, gates, ln_g.reshape(1, -1), ln_b.reshape(1, -1), sgu_w, sgu_b.T,
      wb_bf16, wo_bf16, final_g.reshape(1, -1))


def kernel(x, positions, norm_g, w_in, lam_q1, lam_k1, lam_q2, lam_k2, subln_g, sgu_ln_g, sgu_ln_b,
           sgu_w, sgu_b, w_branch, w_out, final_g):
    batch, seq, _ = x.shape
    depth = norm_g.shape[0]
    assert batch == 1 and seq % C_TILE == 0
    h = x.reshape(seq, D_MODEL)
    cos_t, sin_t = _rope_tables(positions.reshape(seq), seq)
    for l in range(depth):
        lam_init = 0.8 - 0.6 * math.exp(-0.3 * l)
        (qa, ka, va, za, guv, zb, qc, kc, vc, zc, gates) = _inproj(
            h, norm_g[l], w_in[l].astype(BF16), cos_t, sin_t)
        ya = _attn_a(qa, ka, va, za, subln_g[l], lam_q1[l], lam_k1[l], lam_q2[l], lam_k2[l], lam_init)
        yc = _attn_c(qc, kc, vc, zc)
        h = _merge(h, ya, yc, guv, zb, gates, sgu_ln_g[l], sgu_ln_b[l], sgu_w[l], sgu_b[l],
                   w_branch[l].astype(BF16), w_out[l].astype(BF16), final_g,
                   final_norm=(l == depth - 1))
    return h.reshape(batch, seq, D_MODEL)
```

```python
import functools
import math

import jax
import jax.numpy as jnp
from jax import lax
from jax.experimental import pallas as pl
from jax.experimental.pallas import tpu as pltpu

D_MODEL = 1024
A_HEADS = 4
A_HEAD_DIM = 64
A_WIDTH = A_HEADS * 2 * A_HEAD_DIM
B_WIDTH = 512
B_GROUPS = 4
B_GROUP_DIM = B_WIDTH // B_GROUPS
B_CHUNK = 128
C_HEADS = 8
C_HEAD_DIM = 64
C_WIDTH = C_HEADS * C_HEAD_DIM
C_PATTERNS = ((128, 1), (512, 4), (2048, 16))
C_BLOCK = 128
N_BRANCH = 3
BRANCH_WIDTH = 512
ROPE_THETA = 500000.0
ROPE_FRAC = 4
EPS = 1e-6

LANES = 128
VMEM_LIMIT = 56 * 1024 * 1024

_SIZES = (A_WIDTH, A_WIDTH, A_WIDTH, A_WIDTH, 2 * B_WIDTH, B_WIDTH,
          C_WIDTH, C_WIDTH, C_WIDTH, C_WIDTH, N_BRANCH * D_MODEL)
_OFFS = tuple(int(sum(_SIZES[:i])) for i in range(len(_SIZES)))
IN_COLS = int(sum(_SIZES))

F32 = jnp.float32
BF16 = jnp.bfloat16


def _silu(t):
    return t * (1.0 / (1.0 + jnp.exp(-t)))


def _sigmoid(t):
    return 1.0 / (1.0 + jnp.exp(-t))


def _gelu_tanh(t):
    c = math.sqrt(2.0 / math.pi)
    return 0.5 * t * (1.0 + jnp.tanh(c * (t + 0.044715 * (t * t * t))))


def _rope_table_kernel(pos_ref, inv_ref, sgn_ref, cos_ref, sin_ref):
    ang = pos_ref[...].astype(F32) * inv_ref[...]
    cos_ref[...] = jnp.cos(ang)
    sin_ref[...] = jnp.sin(ang) * sgn_ref[...]


def _rope_tables(positions, seq):
    rot = A_HEAD_DIM // ROPE_FRAC
    half = rot // 2
    inv = jnp.power(jnp.float32(ROPE_THETA), -jnp.arange(half, dtype=jnp.float32) * 2.0 / rot)
    zeros = jnp.zeros((A_HEAD_DIM - rot,), F32)
    inv_head = jnp.concatenate([inv, inv, zeros])
    sgn_head = jnp.concatenate([-jnp.ones((half,), F32), jnp.ones((half,), F32), zeros])
    reps = LANES // A_HEAD_DIM
    inv_lane = jnp.tile(inv_head, reps).reshape(1, LANES)
    sgn_lane = jnp.tile(sgn_head, reps).reshape(1, LANES)
    tm = 1024
    return pl.pallas_call(
        _rope_table_kernel,
        grid=(seq // tm,),
        in_specs=[pl.BlockSpec((tm, 1), lambda i: (i, 0)),
                  pl.BlockSpec((1, LANES), lambda i: (0, 0)),
                  pl.BlockSpec((1, LANES), lambda i: (0, 0))],
        out_specs=[pl.BlockSpec((tm, LANES), lambda i: (i, 0)),
                   pl.BlockSpec((tm, LANES), lambda i: (i, 0))],
        out_shape=[jax.ShapeDtypeStruct((seq, LANES), F32)] * 2,
        name="rope_tables",
    )(positions.reshape(seq, 1), inv_lane, sgn_lane)


def _inproj_kernel(x_ref, g_ref, w_ref, cos_ref, sin_ref,
                   qa_ref, ka_ref, va_ref, za_ref, guv_ref, zb_ref,
                   qc_ref, kc_ref, vc_ref, zc_ref, gate_ref):
    x = x_ref[...]
    h = x * lax.rsqrt(jnp.mean(x * x, axis=-1, keepdims=True) + EPS) * g_ref[...]
    h = h.astype(BF16)
    cos = cos_ref[...]
    sin = sin_ref[...]
    tm = x.shape[0]
    lane = lax.broadcasted_iota(jnp.int32, (tm, LANES), 1)
    first_half = (lane % A_HEAD_DIM) < (A_HEAD_DIM // ROPE_FRAC // 2)
    shift = A_HEAD_DIM // ROPE_FRAC // 2

    def proj(lo, n):
        return jnp.dot(h, w_ref[:, lo:lo + n], preferred_element_type=F32)

    def rope_store(seg, out_ref, scale):
        for c in range(out_ref.shape[1] // LANES):
            t = proj(_OFFS[seg] + c * LANES, LANES)
            partner = jnp.where(first_half, pltpu.roll(t, LANES - shift, 1), pltpu.roll(t, shift, 1))
            r = t * cos + partner * sin
            if scale != 1.0:
                r = r * scale
            out_ref[:, c * LANES:(c + 1) * LANES] = r.astype(out_ref.dtype)

    def act_store(seg, out_ref, fn, width=512):
        n = out_ref.shape[1]
        for c in range(n // width):
            t = proj(_OFFS[seg] + c * width, width)
            out_ref[:, c * width:(c + 1) * width] = fn(t).astype(out_ref.dtype)

    ident = lambda t: t
    rope_store(0, qa_ref, LOG2E / math.sqrt(A_HEAD_DIM))
    rope_store(1, ka_ref, 1.0)
    act_store(2, va_ref, ident)
    act_store(3, za_ref, _silu)
    act_store(4, guv_ref, _gelu_tanh)
    act_store(5, zb_ref, _silu)
    rope_store(6, qc_ref, 1.0 / math.sqrt(C_HEAD_DIM))
    rope_store(7, kc_ref, 1.0)
    act_store(8, vc_ref, ident)
    act_store(9, zc_ref, _silu)
    act_store(10, gate_ref, _sigmoid)


def _inproj(x2, norm_g, w_bf16, cos_t, sin_t, tm=256):
    seq = x2.shape[0]
    row = lambda i: (i, 0)
    fixed = lambda i: (0, 0)
    widths = (A_WIDTH, A_WIDTH, A_WIDTH, A_WIDTH, 2 * B_WIDTH, B_WIDTH,
              C_WIDTH, C_WIDTH, C_WIDTH, C_WIDTH, N_BRANCH * D_MODEL)
    dtypes = (BF16, BF16, BF16, BF16, BF16, BF16, F32, F32, F32, BF16, BF16)
    return pl.pallas_call(
        _inproj_kernel,
        grid=(seq // tm,),
        in_specs=[pl.BlockSpec((tm, D_MODEL), row),
                  pl.BlockSpec((1, D_MODEL), fixed),
                  pl.BlockSpec((D_MODEL, IN_COLS), fixed, pipeline_mode=pl.Buffered(1)),
                  pl.BlockSpec((tm, LANES), row),
                  pl.BlockSpec((tm, LANES), row)],
        out_specs=[pl.BlockSpec((tm, w), row) for w in widths],
        out_shape=[jax.ShapeDtypeStruct((seq, w), d) for w, d in zip(widths, dtypes)],
        compiler_params=pltpu.CompilerParams(dimension_semantics=("arbitrary",),
                                             vmem_limit_bytes=VMEM_LIMIT),
        name="inproj",
    )(x2, norm_g.reshape(1, D_MODEL), w_bf16, cos_t, sin_t)


LOG2E = math.log2(math.e)
A_ONES_ROWS = 16
A_QCHUNK = 256
A_PIPE_MIN = 4


def _attn_a_kernel(qt_ref, k_ref, vt_ref, za_ref, g_ref, lq1_ref, lk1_ref, lq2_ref, lk2_ref,
                   o_ref, qz_scr, m_scr, acc_scr, s_a, s_b, p_a, p_b, al_a, al_b,
                   *, tq, tk, lam_init):
    assert tq == tk
    i = pl.program_id(1)
    d = A_HEAD_DIM
    qt = qt_ref[...]
    row = lax.broadcasted_iota(jnp.int32, qt.shape, 0)
    zero = jnp.zeros_like(qt)
    qz_scr[0] = jnp.where(row < d, qt, zero)
    qz_scr[1] = jnp.where(row >= d, qt, zero)
    m_scr[...] = jnp.full(m_scr.shape, -jnp.inf, F32)
    acc_scr[...] = jnp.zeros(acc_scr.shape, F32)

    chunks = [(m, slice(h * A_QCHUNK, (h + 1) * A_QCHUNK)) for h in range(tq // A_QCHUNK) for m in range(2)]

    def scores(j, s_buf, c):
        m, qs = c
        kb = k_ref[pl.ds(pl.multiple_of(j * tk, tk), tk), :]
        s_buf[m, :, qs] = jnp.dot(kb, qz_scr[m, :, qs], preferred_element_type=F32)

    def softmax(s_buf, p_buf, al_buf, masked, c):
        m, qs = c
        s = s_buf[m, :, qs]
        if masked:
            keyi = lax.broadcasted_iota(jnp.int32, s.shape, 0)
            qryi = lax.broadcasted_iota(jnp.int32, s.shape, 1) + qs.start
            s = jnp.where(keyi <= qryi, s, -jnp.inf)
        m_old = m_scr[m, :, qs]
        m_new = jnp.maximum(m_old, jnp.max(s, axis=0, keepdims=True))
        p_buf[m, :, qs] = jnp.exp2((s - m_new).astype(BF16))
        al_buf[m, :, qs] = jnp.exp2(m_old - m_new)
        m_scr[m, :, qs] = m_new

    def values(j, p_buf, al_buf, c):
        m, qs = c
        vtb = vt_ref[j]
        acc_scr[m, :, qs] = (al_buf[m, :, qs] * acc_scr[m, :, qs]
                             + jnp.dot(vtb, p_buf[m, :, qs], preferred_element_type=F32))

    def stage(val=None, sco=None, sm=None):
        for c in chunks:
            if sco is not None:
                scores(*sco, c)
            if val is not None:
                values(*val, c)
            if sm is not None:
                softmax(*sm, c)

    A = (p_a, al_a)
    B = (p_b, al_b)

    stage(sco=(0, s_a))

    @pl.when(i >= 1)
    def _():
        stage(sco=(1, s_b), sm=(s_a, *A, False))

    def pair(t):
        stage(sco=(t, s_a), val=(t - 2, *A), sm=(s_b, *B, False))
        stage(sco=(t + 1, s_b), val=(t - 1, *B), sm=(s_a, *A, False))

    def two_pairs(u, carry):
        pair(2 + 4 * u)
        pair(4 + 4 * u)
        return carry

    n_pairs = jnp.maximum(i - 1, 0) // 2
    lax.fori_loop(0, n_pairs // 2, two_pairs, 0)

    @pl.when(n_pairs % 2 == 1)
    def _():
        pair(2 * n_pairs)

    @pl.when(i % 2 == 0)
    def _():
        @pl.when(i >= 2)
        def _():
            stage(sco=(i, s_a), val=(i - 2, *A), sm=(s_b, *B, False))
            stage(val=(i - 1, *B))
        stage(sm=(s_a, *A, True))
        stage(val=(i, *A))

    @pl.when(i % 2 == 1)
    def _():
        stage(val=(i - 1, *A))
        stage(sm=(s_b, *B, True))
        stage(val=(i, *B))

    lam = (jnp.exp(jnp.sum(lq1_ref[...] * lk1_ref[...], axis=-1, keepdims=True))
           - jnp.exp(jnp.sum(lq2_ref[...] * lk2_ref[...], axis=-1, keepdims=True)) + lam_init)
    dv = 2 * A_HEAD_DIM
    o0 = acc_scr[0, 0:dv, :] * (1.0 / acc_scr[0, dv:dv + 1, :])
    o1 = acc_scr[1, 0:dv, :] * (1.0 / acc_scr[1, dv:dv + 1, :])
    o = o0 - lam * o1
    y = o * lax.rsqrt(jnp.mean(o * o, axis=0, keepdims=True) + EPS) * g_ref[...]
    y = (y * (1.0 - lam_init)).T
    o_ref[...] = (y * za_ref[...].astype(F32)).astype(o_ref.dtype)


def _attn_a(qa, ka, va, za, subln_g, lq1, lk1, lq2, lk2, lam_init, tq=512, tk=512):
    seq = qa.shape[0]
    nk = seq // tk
    qt = qa.T
    vt = va.reshape(nk, tk, A_HEADS, 2 * A_HEAD_DIM).transpose(2, 0, 3, 1)
    vt = jnp.concatenate([vt, jnp.ones((A_HEADS, nk, A_ONES_ROWS, tk), BF16)], axis=2)
    fixed = lambda h, i: (0, 0)
    vec = lambda a: a.reshape(1, -1)
    return pl.pallas_call(
        functools.partial(_attn_a_kernel, tq=tq, tk=tk, lam_init=lam_init),
        grid=(A_HEADS, seq // tq),
        in_specs=[pl.BlockSpec((LANES, tq), lambda h, i: (h, i)),
                  pl.BlockSpec((seq, LANES), lambda h, i: (0, h)),
                  pl.BlockSpec((None, nk, LANES + A_ONES_ROWS, tk), lambda h, i: (h, 0, 0, 0)),
                  pl.BlockSpec((tq, LANES), lambda h, i: (i, h)),
                  pl.BlockSpec((2 * A_HEAD_DIM, 1), fixed),
                  pl.BlockSpec((1, A_HEAD_DIM), fixed),
                  pl.BlockSpec((1, A_HEAD_DIM), fixed),
                  pl.BlockSpec((1, A_HEAD_DIM), fixed),
                  pl.BlockSpec((1, A_HEAD_DIM), fixed)],
        out_specs=pl.BlockSpec((tq, LANES), lambda h, i: (i, h)),
        out_shape=jax.ShapeDtypeStruct((seq, A_WIDTH), BF16),
        scratch_shapes=[pltpu.VMEM((2, LANES, tq), BF16),
                        pltpu.VMEM((2, 1, tq), F32),
                        pltpu.VMEM((2, 2 * A_HEAD_DIM + A_ONES_ROWS, tq), F32),
                        pltpu.VMEM((2, tk, tq), F32), pltpu.VMEM((2, tk, tq), F32),
                        pltpu.VMEM((2, tk, tq), BF16), pltpu.VMEM((2, tk, tq), BF16),
                        pltpu.VMEM((2, 1, tq), F32), pltpu.VMEM((2, 1, tq), F32)],
        compiler_params=pltpu.CompilerParams(dimension_semantics=("arbitrary", "arbitrary"),
                                             vmem_limit_bytes=VMEM_LIMIT),
        name="attn_a",
    )(qt, ka, vt, za, subln_g.reshape(-1, 1), vec(lq1), vec(lk1), vec(lq2), vec(lk2))


C_TILE = C_BLOCK * max(dil for _, dil in C_PATTERNS)
C_UNROLL = 4


def _attn_c_kernel(q_ref, kp_ref, kc_ref, vp_ref, vc_ref, z_ref, o_ref,
                   kk_scr, vv_scr, out_scr, lse_scr):
    t_idx = pl.program_id(1)
    dh = C_HEAD_DIM
    nt = (((1,), (1,)), ((), ()))
    kk_scr[0:C_TILE, :] = kp_ref[...]
    kk_scr[C_TILE:2 * C_TILE, :] = kc_ref[...]
    vv_scr[0:C_TILE, :] = vp_ref[...]
    vv_scr[C_TILE:2 * C_TILE, :] = vc_ref[...]

    qi = lax.broadcasted_iota(jnp.int32, (C_BLOCK, 2 * C_BLOCK), 0)
    ki = lax.broadcasted_iota(jnp.int32, (C_BLOCK, 2 * C_BLOCK), 1)
    has_prev_tile = t_idx > 0

    for p, (window, dil) in enumerate(C_PATTERNS):
        n_back = window // dil
        rel = qi + C_BLOCK - ki
        band = (rel >= 0) & (rel <= n_back)
        span = C_BLOCK * dil

        def block(b, p=p, dil=dil, band=band, span=span):
            u = b // dil
            r = b % dil
            q_start = u * span + r
            k_start = C_TILE + (u - 1) * span + r
            if dil == 1:
                q2 = q_ref[pl.ds(q_start, C_BLOCK), :]
                k2 = kk_scr[pl.ds(k_start, 2 * C_BLOCK), :]
                v2 = vv_scr[pl.ds(k_start, 2 * C_BLOCK), :]
            else:
                q2 = q_ref[pl.ds(q_start, C_BLOCK, stride=dil), :]
                k2 = kk_scr[pl.ds(k_start, 2 * C_BLOCK, stride=dil), :]
                v2 = vv_scr[pl.ds(k_start, 2 * C_BLOCK, stride=dil), :]
            q2 = q2.astype(BF16)
            k2 = k2.astype(BF16)
            v2 = v2.astype(BF16)
            valid = band & ((ki >= C_BLOCK) | (u > 0) | has_prev_tile)
            outs = []
            lses = []
            for hh in range(LANES // dh):
                sl = slice(hh * dh, (hh + 1) * dh)
                s = lax.dot_general(q2[:, sl], k2[:, sl], nt, preferred_element_type=F32)
                s = jnp.where(valid, s, -jnp.inf)
                mx = jnp.max(s, axis=-1, keepdims=True)
                e = jnp.exp(s - mx)
                den = jnp.sum(e, axis=-1, keepdims=True)
                o = jnp.dot(e.astype(BF16), v2[:, sl], preferred_element_type=F32) / den
                outs.append(o)
                lses.append(jnp.broadcast_to(mx + jnp.log(den), (C_BLOCK, dh)))
            o2 = jnp.concatenate(outs, axis=-1)
            l2 = jnp.concatenate(lses, axis=-1)
            if dil == 1:
                out_scr[p, pl.ds(q_start, C_BLOCK), :] = o2
                lse_scr[p, pl.ds(q_start, C_BLOCK), :] = l2
            else:
                out_scr[p, pl.ds(q_start, C_BLOCK, stride=dil), :] = o2
                lse_scr[p, pl.ds(q_start, C_BLOCK, stride=dil), :] = l2

        def blocks(bb, carry, block=block):
            for k in range(C_UNROLL):
                block(bb * C_UNROLL + k)
            return carry

        lax.fori_loop(0, C_TILE // C_BLOCK // C_UNROLL, blocks, 0)

    l0, l1, l2 = lse_scr[0], lse_scr[1], lse_scr[2]
    mx = jnp.maximum(jnp.maximum(l0, l1), l2)
    w0 = jnp.exp(l0 - mx)
    w1 = jnp.exp(l1 - mx)
    w2 = jnp.exp(l2 - mx)
    oc = (w0 * out_scr[0] + w1 * out_scr[1] + w2 * out_scr[2]) / (w0 + w1 + w2)
    o_ref[...] = (oc * z_ref[...].astype(F32)).astype(o_ref.dtype)


def _attn_c(qc, kc, vc, zc):
    seq = qc.shape[0]
    cur = lambda c, t: (t, c)
    prev = lambda c, t: (jnp.maximum(t - 1, 0), c)
    tile = (C_TILE, LANES)
    return pl.pallas_call(
        _attn_c_kernel,
        grid=(C_WIDTH // LANES, seq // C_TILE),
        in_specs=[pl.BlockSpec(tile, cur),
                  pl.BlockSpec(tile, prev), pl.BlockSpec(tile, cur),
                  pl.BlockSpec(tile, prev), pl.BlockSpec(tile, cur),
                  pl.BlockSpec(tile, cur)],
        out_specs=pl.BlockSpec(tile, cur),
        out_shape=jax.ShapeDtypeStruct((seq, C_WIDTH), BF16),
        scratch_shapes=[pltpu.VMEM((2 * C_TILE, LANES), F32),
                        pltpu.VMEM((2 * C_TILE, LANES), F32),
                        pltpu.VMEM((len(C_PATTERNS), C_TILE, LANES), F32),
                        pltpu.VMEM((len(C_PATTERNS), C_TILE, LANES), F32)],
        compiler_params=pltpu.CompilerParams(dimension_semantics=("arbitrary", "arbitrary"),
                                             vmem_limit_bytes=VMEM_LIMIT),
        name="attn_c",
    )(qc, kc, kc, vc, vc, zc)


def _merge_kernel(x_ref, ya_ref, yc_ref, guv_ref, zb_ref, gate_ref, lng_ref, lnb_ref,
                  sw_ref, sb_ref, wb_ref, wo_ref, fg_ref, o_ref, *, final_norm):
    tm = x_ref.shape[0]
    guv = guv_ref[...]
    u = guv[:, :B_WIDTH].astype(F32)
    vb = guv[:, B_WIDTH:].astype(F32)
    mu = jnp.mean(vb, axis=-1, keepdims=True)
    var = jnp.mean(jnp.square(vb - mu), axis=-1, keepdims=True)
    vb = ((vb - mu) * lax.rsqrt(var + EPS) * lng_ref[...] + lnb_ref[...]).astype(BF16)
    ti = lax.broadcasted_iota(jnp.int32, (B_CHUNK, B_CHUNK), 0)
    si = lax.broadcasted_iota(jnp.int32, (B_CHUNK, B_CHUNK), 1)
    causal = si <= ti
    rows = []
    for c in range(tm // B_CHUNK):
        cols = []
        for g in range(B_GROUPS):
            w = jnp.where(causal, sw_ref[g], 0.0).astype(BF16)
            blk = vb[c * B_CHUNK:(c + 1) * B_CHUNK, g * B_GROUP_DIM:(g + 1) * B_GROUP_DIM]
            mixed = jnp.dot(w, blk, preferred_element_type=F32) + sb_ref[:, g:g + 1]
            cols.append(mixed)
        rows.append(jnp.concatenate(cols, axis=-1))
    mixed = jnp.concatenate(rows, axis=0)
    yb = (u * mixed * zb_ref[...].astype(F32)).astype(BF16)

    gates = gate_ref[...]
    merged = jnp.zeros((tm, D_MODEL), F32)
    for n, y in enumerate((ya_ref[...], yb, yc_ref[...])):
        pb = jnp.dot(y, wb_ref[n], preferred_element_type=F32)
        merged = merged + gates[:, n * D_MODEL:(n + 1) * D_MODEL].astype(F32) * pb
    out = x_ref[...] + jnp.dot(merged.astype(BF16), wo_ref[...], preferred_element_type=F32)
    if final_norm:
        out = out * lax.rsqrt(jnp.mean(out * out, axis=-1, keepdims=True) + EPS) * fg_ref[...]
    o_ref[...] = out


def _merge(x2, ya, yc, guv, zb, gates, ln_g, ln_b, sgu_w, sgu_b, wb_bf16, wo_bf16, final_g,
           final_norm, tm=256):
    seq = x2.shape[0]
    row = lambda i: (i, 0)
    fixed2 = lambda i: (0, 0)
    fixed3 = lambda i: (0, 0, 0)
    return pl.pallas_call(
        functools.partial(_merge_kernel, final_norm=final_norm),
        grid=(seq // tm,),
        in_specs=[pl.BlockSpec((tm, D_MODEL), row),
                  pl.BlockSpec((tm, A_WIDTH), row),
                  pl.BlockSpec((tm, C_WIDTH), row),
                  pl.BlockSpec((tm, 2 * B_WIDTH), row),
                  pl.BlockSpec((tm, B_WIDTH), row),
                  pl.BlockSpec((tm, N_BRANCH * D_MODEL), row),
                  pl.BlockSpec((1, B_WIDTH), fixed2),
                  pl.BlockSpec((1, B_WIDTH), fixed2),
                  pl.BlockSpec((B_GROUPS, B_CHUNK, B_CHUNK), fixed3),
                  pl.BlockSpec((B_CHUNK, B_GROUPS), fixed2),
                  pl.BlockSpec((N_BRANCH, BRANCH_WIDTH, D_MODEL), fixed3),
                  pl.BlockSpec((D_MODEL, D_MODEL), fixed2),
                  pl.BlockSpec((1, D_MODEL), fixed2)],
        out_specs=pl.BlockSpec((tm, D_MODEL), row),
        out_shape=jax.ShapeDtypeStruct((seq, D_MODEL), F32),
        compiler_params=pltpu.CompilerParams(dimension_semantics=("arbitrary",),
                                             vmem_limit_bytes=VMEM_LIMIT),
        name="merge",
    )(x2, ya, yc, guv, zb, gates, ln_g.reshape(1, -1), ln_b.reshape(1, -1), sgu_w, sgu_b.T,
      wb_bf16, wo_bf16, final_g.reshape(1, -1))


def kernel(x, positions, norm_g, w_in, lam_q1, lam_k1, lam_q2, lam_k2, subln_g, sgu_ln_g, sgu_ln_b,
           sgu_w, sgu_b, w_branch, w_out, final_g):
    batch, seq, _ = x.shape
    depth = norm_g.shape[0]
    assert batch == 1 and seq % C_TILE == 0
    h = x.reshape(seq, D_MODEL)
    cos_t, sin_t = _rope_tables(positions.reshape(seq), seq)
    for l in range(depth):
        lam_init = 0.8 - 0.6 * math.exp(-0.3 * l)
        (qa, ka, va, za, guv, zb, qc, kc, vc, zc, gates) = _inproj(
            h, norm_g[l], w_in[l].astype(BF16), cos_t, sin_t)
        ya = _attn_a(qa, ka, va, za, subln_g[l], lam_q1[l], lam_k1[l], lam_q2[l], lam_k2[l], lam_init)
        yc = _attn_c(qc, kc, vc, zc)
        h = _merge(h, ya, yc, guv, zb, gates, sgu_ln_g[l], sgu_ln_b[l], sgu_w[l], sgu_b[l],
                   w_branch[l].astype(BF16), w_out[l].astype(BF16), final_g,
                   final_norm=(l == depth - 1))
    return h.reshape(batch, seq, D_MODEL)
```

```python
import functools
import math

import jax
import jax.numpy as jnp
import numpy as np
from jax import lax
from jax.experimental import pallas as pl
from jax.experimental.pallas import tpu as pltpu

D_MODEL = 1024
A_HEADS = 4
A_HEAD_DIM = 64
A_WIDTH = A_HEADS * 2 * A_HEAD_DIM
B_WIDTH = 512
B_GROUPS = 4
B_GROUP_DIM = B_WIDTH // B_GROUPS
B_CHUNK = 128
C_HEADS = 8
C_HEAD_DIM = 64
C_WIDTH = C_HEADS * C_HEAD_DIM
C_PATTERNS = ((128, 1), (512, 4), (2048, 16))
C_BLOCK = 128
N_BRANCH = 3
BRANCH_WIDTH = 512
ROPE_THETA = 500000.0
ROPE_FRAC = 4
EPS = 1e-6

LANES = 128
VMEM_LIMIT = 56 * 1024 * 1024

_SIZES = (A_WIDTH, A_WIDTH, A_WIDTH, A_WIDTH, 2 * B_WIDTH, B_WIDTH,
          C_WIDTH, C_WIDTH, C_WIDTH, C_WIDTH, N_BRANCH * D_MODEL)
_OFFS = tuple(int(sum(_SIZES[:i])) for i in range(len(_SIZES)))
IN_COLS = int(sum(_SIZES))

F32 = jnp.float32
BF16 = jnp.bfloat16


def _silu(t):
    return t * (1.0 / (1.0 + jnp.exp(-t)))


def _sigmoid(t):
    return 1.0 / (1.0 + jnp.exp(-t))


def _gelu_tanh(t):
    c = math.sqrt(2.0 / math.pi)
    return 0.5 * t * (1.0 + jnp.tanh(c * (t + 0.044715 * (t * t * t))))


def _rope_table_kernel(pos_ref, inv_ref, sgn_ref, cos_ref, sin_ref):
    ang = pos_ref[...].astype(F32) * inv_ref[...]
    cos_ref[...] = jnp.cos(ang)
    sin_ref[...] = jnp.sin(ang) * sgn_ref[...]


def _rope_tables(positions, seq):
    rot = A_HEAD_DIM // ROPE_FRAC
    half = rot // 2
    inv = jnp.power(jnp.float32(ROPE_THETA), -jnp.arange(half, dtype=jnp.float32) * 2.0 / rot)
    zeros = jnp.zeros((A_HEAD_DIM - rot,), F32)
    inv_head = jnp.concatenate([inv, inv, zeros])
    sgn_head = jnp.concatenate([-jnp.ones((half,), F32), jnp.ones((half,), F32), zeros])
    reps = LANES // A_HEAD_DIM
    inv_lane = jnp.tile(inv_head, reps).reshape(1, LANES)
    sgn_lane = jnp.tile(sgn_head, reps).reshape(1, LANES)
    tm = 1024
    return pl.pallas_call(
        _rope_table_kernel,
        grid=(seq // tm,),
        in_specs=[pl.BlockSpec((tm, 1), lambda i: (i, 0)),
                  pl.BlockSpec((1, LANES), lambda i: (0, 0)),
                  pl.BlockSpec((1, LANES), lambda i: (0, 0))],
        out_specs=[pl.BlockSpec((tm, LANES), lambda i: (i, 0)),
                   pl.BlockSpec((tm, LANES), lambda i: (i, 0))],
        out_shape=[jax.ShapeDtypeStruct((seq, LANES), F32)] * 2,
        name="rope_tables",
    )(positions.reshape(seq, 1), inv_lane, sgn_lane)


def _inproj_kernel(x_ref, g_ref, w_ref, cos_ref, sin_ref,
                   qa_ref, ka_ref, va_ref, za_ref, guv_ref, zb_ref,
                   qc_ref, kc_ref, vc_ref, zc_ref, gate_ref):
    x = x_ref[...]
    h = x * lax.rsqrt(jnp.mean(x * x, axis=-1, keepdims=True) + EPS) * g_ref[...]
    h = h.astype(BF16)
    cos = cos_ref[...]
    sin = sin_ref[...]
    tm = x.shape[0]
    lane = lax.broadcasted_iota(jnp.int32, (tm, LANES), 1)
    first_half = (lane % A_HEAD_DIM) < (A_HEAD_DIM // ROPE_FRAC // 2)
    shift = A_HEAD_DIM // ROPE_FRAC // 2

    def proj(lo, n):
        return jnp.dot(h, w_ref[:, lo:lo + n], preferred_element_type=F32)

    def rope_store(seg, out_ref, scale):
        for c in range(out_ref.shape[1] // LANES):
            t = proj(_OFFS[seg] + c * LANES, LANES)
            partner = jnp.where(first_half, pltpu.roll(t, LANES - shift, 1), pltpu.roll(t, shift, 1))
            r = t * cos + partner * sin
            if scale != 1.0:
                r = r * scale
            out_ref[:, c * LANES:(c + 1) * LANES] = r.astype(out_ref.dtype)

    def act_store(seg, out_ref, fn, width=512):
        n = out_ref.shape[1]
        for c in range(n // width):
            t = proj(_OFFS[seg] + c * width, width)
            out_ref[:, c * width:(c + 1) * width] = fn(t).astype(out_ref.dtype)

    ident = lambda t: t
    rope_store(0, qa_ref, LOG2E / math.sqrt(A_HEAD_DIM))
    rope_store(1, ka_ref, 1.0)
    act_store(2, va_ref, ident)
    act_store(3, za_ref, _silu)
    act_store(4, guv_ref, _gelu_tanh)
    act_store(5, zb_ref, _silu)
    rope_store(6, qc_ref, LOG2E / math.sqrt(C_HEAD_DIM))
    rope_store(7, kc_ref, 1.0)
    act_store(8, vc_ref, ident)
    act_store(9, zc_ref, _silu)
    act_store(10, gate_ref, _sigmoid)


def _inproj(x2, norm_g, w_bf16, cos_t, sin_t, tm=256):
    seq = x2.shape[0]
    row = lambda i: (i, 0)
    fixed = lambda i: (0, 0)
    widths = (A_WIDTH, A_WIDTH, A_WIDTH, A_WIDTH, 2 * B_WIDTH, B_WIDTH,
              C_WIDTH, C_WIDTH, C_WIDTH, C_WIDTH, N_BRANCH * D_MODEL)
    dtypes = (BF16, BF16, BF16, BF16, BF16, BF16, F32, F32, F32, BF16, BF16)
    return pl.pallas_call(
        _inproj_kernel,
        grid=(seq // tm,),
        in_specs=[pl.BlockSpec((tm, D_MODEL), row),
                  pl.BlockSpec((1, D_MODEL), fixed),
                  pl.BlockSpec((D_MODEL, IN_COLS), fixed, pipeline_mode=pl.Buffered(1)),
                  pl.BlockSpec((tm, LANES), row),
                  pl.BlockSpec((tm, LANES), row)],
        out_specs=[pl.BlockSpec((tm, w), row) for w in widths],
        out_shape=[jax.ShapeDtypeStruct((seq, w), d) for w, d in zip(widths, dtypes)],
        compiler_params=pltpu.CompilerParams(dimension_semantics=("arbitrary",),
                                             vmem_limit_bytes=VMEM_LIMIT),
        name="inproj",
    )(x2, norm_g.reshape(1, D_MODEL), w_bf16, cos_t, sin_t)


LOG2E = math.log2(math.e)
A_ONES_ROWS = 16
A_QCHUNK = 256


def _attn_a_kernel(qt_ref, k_ref, vt_ref, za_ref, g_ref, lq1_ref, lk1_ref, lq2_ref, lk2_ref,
                   o_ref, qz_scr, m_scr, acc_scr, s_a, s_b, p_a, p_b, al_a, al_b,
                   *, tq, tk, lam_init):
    assert tq == tk
    i = pl.program_id(1)
    d = A_HEAD_DIM
    qt = qt_ref[...]
    row = lax.broadcasted_iota(jnp.int32, qt.shape, 0)
    zero = jnp.zeros_like(qt)
    qz_scr[0] = jnp.where(row < d, qt, zero)
    qz_scr[1] = jnp.where(row >= d, qt, zero)
    m_scr[...] = jnp.full(m_scr.shape, -jnp.inf, F32)
    acc_scr[...] = jnp.zeros(acc_scr.shape, F32)

    chunks = [(m, slice(h * A_QCHUNK, (h + 1) * A_QCHUNK)) for h in range(tq // A_QCHUNK) for m in range(2)]

    def scores(j, s_buf, c):
        m, qs = c
        kb = k_ref[pl.ds(pl.multiple_of(j * tk, tk), tk), :]
        s_buf[m, :, qs] = jnp.dot(kb, qz_scr[m, :, qs], preferred_element_type=F32)

    def softmax(s_buf, p_buf, al_buf, masked, c):
        m, qs = c
        s = s_buf[m, :, qs]
        if masked:
            keyi = lax.broadcasted_iota(jnp.int32, s.shape, 0)
            qryi = lax.broadcasted_iota(jnp.int32, s.shape, 1) + qs.start
            s = jnp.where(keyi <= qryi, s, -jnp.inf)
        m_old = m_scr[m, :, qs]
        m_new = jnp.maximum(m_old, jnp.max(s, axis=0, keepdims=True))
        p_buf[m, :, qs] = jnp.exp2(s - m_new).astype(BF16)
        al_buf[m, :, qs] = jnp.exp2(m_old - m_new)
        m_scr[m, :, qs] = m_new

    def values(j, p_buf, al_buf, c):
        m, qs = c
        vtb = vt_ref[j]
        acc_scr[m, :, qs] = (al_buf[m, :, qs] * acc_scr[m, :, qs]
                             + jnp.dot(vtb, p_buf[m, :, qs], preferred_element_type=F32))

    def stage(val=None, sco=None, sm=None):
        for c in chunks:
            if sco is not None:
                scores(*sco, c)
            if val is not None:
                values(*val, c)
            if sm is not None:
                softmax(*sm, c)

    A = (p_a, al_a)
    B = (p_b, al_b)

    stage(sco=(0, s_a))

    @pl.when(i >= 1)
    def _():
        stage(sco=(1, s_b), sm=(s_a, *A, False))

    def pair(t):
        stage(sco=(t, s_a), val=(t - 2, *A), sm=(s_b, *B, False))
        stage(sco=(t + 1, s_b), val=(t - 1, *B), sm=(s_a, *A, False))

    def two_pairs(u, carry):
        pair(2 + 4 * u)
        pair(4 + 4 * u)
        return carry

    n_pairs = jnp.maximum(i - 1, 0) // 2
    lax.fori_loop(0, n_pairs // 2, two_pairs, 0)

    @pl.when(n_pairs % 2 == 1)
    def _():
        pair(2 * n_pairs)

    @pl.when(i % 2 == 0)
    def _():
        @pl.when(i >= 2)
        def _():
            stage(sco=(i, s_a), val=(i - 2, *A), sm=(s_b, *B, False))
            stage(val=(i - 1, *B))
        stage(sm=(s_a, *A, True))
        stage(val=(i, *A))

    @pl.when(i % 2 == 1)
    def _():
        stage(val=(i - 1, *A))
        stage(sm=(s_b, *B, True))
        stage(val=(i, *B))

    lam = (jnp.exp(jnp.sum(lq1_ref[...] * lk1_ref[...], axis=-1, keepdims=True))
           - jnp.exp(jnp.sum(lq2_ref[...] * lk2_ref[...], axis=-1, keepdims=True)) + lam_init)
    dv = 2 * A_HEAD_DIM
    o0 = acc_scr[0, 0:dv, :] * (1.0 / acc_scr[0, dv:dv + 1, :])
    o1 = acc_scr[1, 0:dv, :] * (1.0 / acc_scr[1, dv:dv + 1, :])
    o = o0 - lam * o1
    y = o * lax.rsqrt(jnp.mean(o * o, axis=0, keepdims=True) + EPS) * g_ref[...]
    y = (y * (1.0 - lam_init)).T
    o_ref[...] = (y * za_ref[...].astype(F32)).astype(o_ref.dtype)


def _attn_a(qa, ka, va, za, subln_g, lq1, lk1, lq2, lk2, lam_init, tq=512, tk=512):
    seq = qa.shape[0]
    nk = seq // tk
    qt = qa.T
    vt = va.reshape(nk, tk, A_HEADS, 2 * A_HEAD_DIM).transpose(2, 0, 3, 1)
    vt = jnp.concatenate([vt, jnp.ones((A_HEADS, nk, A_ONES_ROWS, tk), BF16)], axis=2)
    fixed = lambda h, i: (0, 0)
    vec = lambda a: a.reshape(1, -1)
    return pl.pallas_call(
        functools.partial(_attn_a_kernel, tq=tq, tk=tk, lam_init=lam_init),
        grid=(A_HEADS, seq // tq),
        in_specs=[pl.BlockSpec((LANES, tq), lambda h, i: (h, i)),
                  pl.BlockSpec((seq, LANES), lambda h, i: (0, h)),
                  pl.BlockSpec((None, nk, LANES + A_ONES_ROWS, tk), lambda h, i: (h, 0, 0, 0)),
                  pl.BlockSpec((tq, LANES), lambda h, i: (i, h)),
                  pl.BlockSpec((2 * A_HEAD_DIM, 1), fixed),
                  pl.BlockSpec((1, A_HEAD_DIM), fixed),
                  pl.BlockSpec((1, A_HEAD_DIM), fixed),
                  pl.BlockSpec((1, A_HEAD_DIM), fixed),
                  pl.BlockSpec((1, A_HEAD_DIM), fixed)],
        out_specs=pl.BlockSpec((tq, LANES), lambda h, i: (i, h)),
        out_shape=jax.ShapeDtypeStruct((seq, A_WIDTH), BF16),
        scratch_shapes=[pltpu.VMEM((2, LANES, tq), BF16),
                        pltpu.VMEM((2, 1, tq), F32),
                        pltpu.VMEM((2, 2 * A_HEAD_DIM + A_ONES_ROWS, tq), F32),
                        pltpu.VMEM((2, tk, tq), F32), pltpu.VMEM((2, tk, tq), F32),
                        pltpu.VMEM((2, tk, tq), BF16), pltpu.VMEM((2, tk, tq), BF16),
                        pltpu.VMEM((2, 1, tq), F32), pltpu.VMEM((2, 1, tq), F32)],
        compiler_params=pltpu.CompilerParams(dimension_semantics=("arbitrary", "arbitrary"),
                                             vmem_limit_bytes=VMEM_LIMIT),
        name="attn_a",
    )(qt, ka, vt, za, subln_g.reshape(-1, 1), vec(lq1), vec(lk1), vec(lq2), vec(lk2))


C_TILE = C_BLOCK * max(dil for _, dil in C_PATTERNS)
C_UNROLL = 8


def _attn_c_kernel(q_ref, kp_ref, kc_ref, vp_ref, vc_ref, z_ref, bias_ref, o_ref,
                   kk_scr, vv_scr, out_scr, lse_scr):
    t_idx = pl.program_id(1)
    kk_scr[0:C_TILE, :] = kp_ref[...]
    kk_scr[C_TILE:2 * C_TILE, :] = kc_ref[...]
    vv_scr[0:C_TILE, :] = vp_ref[...]
    vv_scr[C_TILE:2 * C_TILE, :] = vc_ref[...]

    head0 = lax.broadcasted_iota(jnp.int32, (LANES, C_BLOCK), 0) < C_HEAD_DIM

    def pick(t):
        return jnp.where(head0, t[:, :C_BLOCK], t[:, C_BLOCK:])

    for p, (window, dil) in enumerate(C_PATTERNS):
        assert window // dil == C_BLOCK
        span = C_BLOCK * dil

        def rows(b, dil=dil, span=span):
            u = b // dil
            r = b % dil
            q_start = u * span + r
            k_start = C_TILE + (u - 1) * span + r
            first = jnp.logical_and(u == 0, t_idx == 0).astype(jnp.int32)
            if dil == 1:
                return pl.ds(q_start, C_BLOCK), pl.ds(k_start, 2 * C_BLOCK), first
            return (pl.ds(q_start, C_BLOCK, stride=dil), pl.ds(k_start, 2 * C_BLOCK, stride=dil), first)

        def blocks(bb, carry, p=p, rows=rows):
            idx = [rows(bb * C_UNROLL + k) for k in range(C_UNROLL)]
            q_both, v_t, s, e, stats, o_all = [], [], [], [], [], []
            for q_rows, k_rows, _ in idx:
                qt = q_ref[q_rows, :].T
                zero = jnp.zeros_like(qt)
                q_both.append(jnp.concatenate([jnp.where(head0, qt, zero), jnp.where(head0, zero, qt)],
                                              axis=1).astype(BF16))
                v_t.append(vv_scr[k_rows, :].T.astype(BF16))
            for k, (_, k_rows, first) in enumerate(idx):
                sk = jnp.dot(kk_scr[k_rows, :].astype(BF16), q_both[k], preferred_element_type=F32)
                s.append(sk + bias_ref[first])
            for k in range(C_UNROLL):
                mx = jnp.max(s[k], axis=0, keepdims=True)
                ek = jnp.exp2(s[k] - mx)
                den = jnp.sum(ek, axis=0, keepdims=True)
                e.append(ek.astype(BF16))
                stats.append((1.0 / den, mx + jnp.log2(den)))
            for k in range(C_UNROLL):
                o_all.append(jnp.dot(v_t[k], e[k], preferred_element_type=F32))
            wide = (LANES, 2 * C_BLOCK)
            for k, (q_rows, _, _) in enumerate(idx):
                inv, lse = stats[k]
                o_t = pick(o_all[k]) * pick(jnp.broadcast_to(inv, wide))
                l_t = pick(jnp.broadcast_to(lse, wide))
                out_scr[p, q_rows, :] = o_t.T
                lse_scr[p, q_rows, :] = l_t.T
            return carry

        lax.fori_loop(0, C_TILE // C_BLOCK // C_UNROLL, blocks, 0)

    l0, l1, l2 = lse_scr[0], lse_scr[1], lse_scr[2]
    mx = jnp.maximum(jnp.maximum(l0, l1), l2)
    w0 = jnp.exp2(l0 - mx)
    w1 = jnp.exp2(l1 - mx)
    w2 = jnp.exp2(l2 - mx)
    oc = (w0 * out_scr[0] + w1 * out_scr[1] + w2 * out_scr[2]) / (w0 + w1 + w2)
    o_ref[...] = (oc * z_ref[...].astype(F32)).astype(o_ref.dtype)


def _attn_c_bias():
    ki = np.arange(2 * C_BLOCK)[:, None]
    qi = np.tile(np.arange(C_BLOCK), LANES // C_HEAD_DIM)[None, :]
    rel = qi + C_BLOCK - ki
    band = (rel >= 0) & (rel <= C_BLOCK)
    masks = np.stack([band, band & (ki >= C_BLOCK)])
    return jnp.asarray(np.where(masks, 0.0, -np.inf), dtype=F32)


def _attn_c(qc, kc, vc, zc):
    seq = qc.shape[0]
    cur = lambda c, t: (t, c)
    prev = lambda c, t: (jnp.maximum(t - 1, 0), c)
    tile = (C_TILE, LANES)
    return pl.pallas_call(
        _attn_c_kernel,
        grid=(C_WIDTH // LANES, seq // C_TILE),
        in_specs=[pl.BlockSpec(tile, cur),
                  pl.BlockSpec(tile, prev), pl.BlockSpec(tile, cur),
                  pl.BlockSpec(tile, prev), pl.BlockSpec(tile, cur),
                  pl.BlockSpec(tile, cur),
                  pl.BlockSpec((2, 2 * C_BLOCK, 2 * C_BLOCK), lambda c, t: (0, 0, 0))],
        out_specs=pl.BlockSpec(tile, cur),
        out_shape=jax.ShapeDtypeStruct((seq, C_WIDTH), BF16),
        scratch_shapes=[pltpu.VMEM((2 * C_TILE, LANES), F32),
                        pltpu.VMEM((2 * C_TILE, LANES), F32),
                        pltpu.VMEM((len(C_PATTERNS), C_TILE, LANES), F32),
                        pltpu.VMEM((len(C_PATTERNS), C_TILE, LANES), F32)],
        compiler_params=pltpu.CompilerParams(dimension_semantics=("arbitrary", "arbitrary"),
                                             vmem_limit_bytes=VMEM_LIMIT),
        name="attn_c",
    )(qc, kc, kc, vc, vc, zc, _attn_c_bias())


def _merge_kernel(x_ref, ya_ref, yc_ref, guv_ref, zb_ref, gate_ref, lng_ref, lnb_ref,
                  sw_ref, sb_ref, wb_ref, wo_ref, fg_ref, o_ref, *, final_norm):
    tm = x_ref.shape[0]
    guv = guv_ref[...]
    u = guv[:, :B_WIDTH].astype(F32)
    vb = guv[:, B_WIDTH:].astype(F32)
    mu = jnp.mean(vb, axis=-1, keepdims=True)
    var = jnp.mean(jnp.square(vb - mu), axis=-1, keepdims=True)
    vb = ((vb - mu) * lax.rsqrt(var + EPS) * lng_ref[...] + lnb_ref[...]).astype(BF16)
    ti = lax.broadcasted_iota(jnp.int32, (B_CHUNK, B_CHUNK), 0)
    si = lax.broadcasted_iota(jnp.int32, (B_CHUNK, B_CHUNK), 1)
    causal = si <= ti
    rows = []
    for c in range(tm // B_CHUNK):
        cols = []
        for g in range(B_GROUPS):
            w = jnp.where(causal, sw_ref[g], 0.0).astype(BF16)
            blk = vb[c * B_CHUNK:(c + 1) * B_CHUNK, g * B_GROUP_DIM:(g + 1) * B_GROUP_DIM]
            mixed = jnp.dot(w, blk, preferred_element_type=F32) + sb_ref[:, g:g + 1]
            cols.append(mixed)
        rows.append(jnp.concatenate(cols, axis=-1))
    mixed = jnp.concatenate(rows, axis=0)
    yb = (u * mixed * zb_ref[...].astype(F32)).astype(BF16)

    gates = gate_ref[...]
    merged = jnp.zeros((tm, D_MODEL), F32)
    for n, y in enumerate((ya_ref[...], yb, yc_ref[...])):
        pb = jnp.dot(y, wb_ref[n], preferred_element_type=F32)
        merged = merged + gates[:, n * D_MODEL:(n + 1) * D_MODEL].astype(F32) * pb
    out = x_ref[...] + jnp.dot(merged.astype(BF16), wo_ref[...], preferred_element_type=F32)
    if final_norm:
        out = out * lax.rsqrt(jnp.mean(out * out, axis=-1, keepdims=True) + EPS) * fg_ref[...]
    o_ref[...] = out


def _merge(x2, ya, yc, guv, zb, gates, ln_g, ln_b, sgu_w, sgu_b, wb_bf16, wo_bf16, final_g,
           final_norm, tm=256):
    seq = x2.shape[0]
    row = lambda i: (i, 0)
    fixed2 = lambda i: (0, 0)
    fixed3 = lambda i: (0, 0, 0)
    return pl.pallas_call(
        functools.partial(_merge_kernel, final_norm=final_norm),
        grid=(seq // tm,),
        in_specs=[pl.BlockSpec((tm, D_MODEL), row),
                  pl.BlockSpec((tm, A_WIDTH), row),
                  pl.BlockSpec((tm, C_WIDTH), row),
                  pl.BlockSpec((tm, 2 * B_WIDTH), row),
                  pl.BlockSpec((tm, B_WIDTH), row),
                  pl.BlockSpec((tm, N_BRANCH * D_MODEL), row),
                  pl.BlockSpec((1, B_WIDTH), fixed2),
                  pl.BlockSpec((1, B_WIDTH), fixed2),
                  pl.BlockSpec((B_GROUPS, B_CHUNK, B_CHUNK), fixed3),
                  pl.BlockSpec((B_CHUNK, B_GROUPS), fixed2),
                  pl.BlockSpec((N_BRANCH, BRANCH_WIDTH, D_MODEL), fixed3),
                  pl.BlockSpec((D_MODEL, D_MODEL), fixed2),
                  pl.BlockSpec((1, D_MODEL), fixed2)],
        out_specs=pl.BlockSpec((tm, D_MODEL), row),
        out_shape=jax.ShapeDtypeStruct((seq, D_MODEL), F32),
        compiler_params=pltpu.CompilerParams(dimension_semantics=("arbitrary",),
                                             vmem_limit_bytes=VMEM_LIMIT),
        name="merge",
    )(x2, ya, yc, guv, zb, gates, ln_g.reshape(1, -1), ln_b.reshape(1, -1), sgu_w, sgu_b.T,
      wb_bf16, wo_bf16, final_g.reshape(1, -1))


def kernel(x, positions, norm_g, w_in, lam_q1, lam_k1, lam_q2, lam_k2, subln_g, sgu_ln_g, sgu_ln_b,
           sgu_w, sgu_b, w_branch, w_out, final_g):
    batch, seq, _ = x.shape
    depth = norm_g.shape[0]
    assert batch == 1 and seq % C_TILE == 0
    h = x.reshape(seq, D_MODEL)
    cos_t, sin_t = _rope_tables(positions.reshape(seq), seq)
    for l in range(depth):
        lam_init = 0.8 - 0.6 * math.exp(-0.3 * l)
        (qa, ka, va, za, guv, zb, qc, kc, vc, zc, gates) = _inproj(
            h, norm_g[l], w_in[l].astype(BF16), cos_t, sin_t)
        ya = _attn_a(qa, ka, va, za, subln_g[l], lam_q1[l], lam_k1[l], lam_q2[l], lam_k2[l], lam_init)
        yc = _attn_c(qc, kc, vc, zc)
        h = _merge(h, ya, yc, guv, zb, gates, sgu_ln_g[l], sgu_ln_b[l], sgu_w[l], sgu_b[l],
                   w_branch[l].astype(BF16), w_out[l].astype(BF16), final_g,
                   final_norm=(l == depth - 1))
    return h.reshape(batch, seq, D_MODEL)
```

```python
import functools
import math

import jax
import jax.numpy as jnp
import numpy as np
from jax import lax
from jax.experimental import pallas as pl
from jax.experimental.pallas import tpu as pltpu

D_MODEL = 1024
A_HEADS = 4
A_HEAD_DIM = 64
A_WIDTH = A_HEADS * 2 * A_HEAD_DIM
B_WIDTH = 512
B_GROUPS = 4
B_GROUP_DIM = B_WIDTH // B_GROUPS
B_CHUNK = 128
C_HEADS = 8
C_HEAD_DIM = 64
C_WIDTH = C_HEADS * C_HEAD_DIM
C_PATTERNS = ((128, 1), (512, 4), (2048, 16))
C_BLOCK = 128
N_BRANCH = 3
BRANCH_WIDTH = 512
ROPE_THETA = 500000.0
ROPE_FRAC = 4
EPS = 1e-6

LANES = 128
VMEM_LIMIT = 56 * 1024 * 1024

_SIZES = (A_WIDTH, A_WIDTH, A_WIDTH, A_WIDTH, 2 * B_WIDTH, B_WIDTH,
          C_WIDTH, C_WIDTH, C_WIDTH, C_WIDTH, N_BRANCH * D_MODEL)
_OFFS = tuple(int(sum(_SIZES[:i])) for i in range(len(_SIZES)))
IN_COLS = int(sum(_SIZES))

F32 = jnp.float32
BF16 = jnp.bfloat16


def _silu(t):
    return t * (1.0 / (1.0 + jnp.exp(-t)))


def _sigmoid(t):
    return 1.0 / (1.0 + jnp.exp(-t))


def _gelu_tanh(t):
    c = math.sqrt(2.0 / math.pi)
    return 0.5 * t * (1.0 + jnp.tanh(c * (t + 0.044715 * (t * t * t))))


def _rope_table_kernel(pos_ref, inv_ref, sgn_ref, cos_ref, sin_ref):
    ang = pos_ref[...].astype(F32) * inv_ref[...]
    cos_ref[...] = jnp.cos(ang)
    sin_ref[...] = jnp.sin(ang) * sgn_ref[...]


def _rope_tables(positions, seq):
    rot = A_HEAD_DIM // ROPE_FRAC
    half = rot // 2
    inv = jnp.power(jnp.float32(ROPE_THETA), -jnp.arange(half, dtype=jnp.float32) * 2.0 / rot)
    zeros = jnp.zeros((A_HEAD_DIM - rot,), F32)
    inv_head = jnp.concatenate([inv, inv, zeros])
    sgn_head = jnp.concatenate([-jnp.ones((half,), F32), jnp.ones((half,), F32), zeros])
    reps = LANES // A_HEAD_DIM
    inv_lane = jnp.tile(inv_head, reps).reshape(1, LANES)
    sgn_lane = jnp.tile(sgn_head, reps).reshape(1, LANES)
    tm = 1024
    return pl.pallas_call(
        _rope_table_kernel,
        grid=(seq // tm,),
        in_specs=[pl.BlockSpec((tm, 1), lambda i: (i, 0)),
                  pl.BlockSpec((1, LANES), lambda i: (0, 0)),
                  pl.BlockSpec((1, LANES), lambda i: (0, 0))],
        out_specs=[pl.BlockSpec((tm, LANES), lambda i: (i, 0)),
                   pl.BlockSpec((tm, LANES), lambda i: (i, 0))],
        out_shape=[jax.ShapeDtypeStruct((seq, LANES), F32)] * 2,
        name="rope_tables",
    )(positions.reshape(seq, 1), inv_lane, sgn_lane)


def _inproj_kernel(x_ref, g_ref, w_ref, cos_ref, sin_ref,
                   qt_ref, ka_ref, vt_ref, za_ref, guv_ref, zb_ref,
                   qc_ref, kc_ref, vc_ref, zc_ref, gate_ref):
    x = x_ref[...]
    h = x * lax.rsqrt(jnp.mean(x * x, axis=-1, keepdims=True) + EPS) * g_ref[...]
    h = h.astype(BF16)
    cos = cos_ref[...]
    sin = sin_ref[...]
    tm = x.shape[0]
    lane = lax.broadcasted_iota(jnp.int32, (tm, LANES), 1)
    first_half = (lane % A_HEAD_DIM) < (A_HEAD_DIM // ROPE_FRAC // 2)
    shift = A_HEAD_DIM // ROPE_FRAC // 2

    def proj(lo, n):
        return jnp.dot(h, w_ref[:, lo:lo + n], preferred_element_type=F32)

    def rope_store(seg, out_ref, scale, width=A_WIDTH, transposed=False):
        full = proj(_OFFS[seg], width)
        for c in range(width // LANES):
            t = full[:, c * LANES:(c + 1) * LANES]
            partner = jnp.where(first_half, pltpu.roll(t, LANES - shift, 1), pltpu.roll(t, shift, 1))
            r = t * cos + partner * sin
            if scale != 1.0:
                r = r * scale
            if transposed:
                out_ref[c * LANES:(c + 1) * LANES, :] = r.T.astype(out_ref.dtype)
            else:
                out_ref[:, c * LANES:(c + 1) * LANES] = r.astype(out_ref.dtype)

    def act_store(seg, out_ref, fn, width=512):
        n = out_ref.shape[1]
        for c in range(n // width):
            t = proj(_OFFS[seg] + c * width, width)
            out_ref[:, c * width:(c + 1) * width] = fn(t).astype(out_ref.dtype)

    ident = lambda t: t
    rope_store(0, qt_ref, LOG2E / math.sqrt(A_HEAD_DIM), transposed=True)
    rope_store(1, ka_ref, 1.0)
    va = proj(_OFFS[2], A_WIDTH)
    dv = 2 * A_HEAD_DIM
    for hd in range(A_HEADS):
        vt_ref[hd, 0:dv, :] = va[:, hd * dv:(hd + 1) * dv].T.astype(vt_ref.dtype)
        vt_ref[hd, dv:dv + A_ONES_ROWS, :] = jnp.ones((A_ONES_ROWS, tm), vt_ref.dtype)
    act_store(3, za_ref, _silu)
    act_store(4, guv_ref, _gelu_tanh)
    act_store(5, zb_ref, _silu)
    rope_store(6, qc_ref, LOG2E / math.sqrt(C_HEAD_DIM))
    rope_store(7, kc_ref, 1.0)
    act_store(8, vc_ref, ident)
    act_store(9, zc_ref, _silu)
    act_store(10, gate_ref, _sigmoid)


def _inproj(x2, norm_g, w_bf16, cos_t, sin_t, tm=256, tk=512):
    seq = x2.shape[0]
    row = lambda i: (i, 0)
    fixed = lambda i: (0, 0)
    per_kv = tk // tm
    dv = 2 * A_HEAD_DIM + A_ONES_ROWS
    widths = (A_WIDTH, A_WIDTH, A_WIDTH, A_WIDTH, 2 * B_WIDTH, B_WIDTH,
              C_WIDTH, C_WIDTH, C_WIDTH, C_WIDTH, N_BRANCH * D_MODEL)
    dtypes = (BF16, BF16, BF16, BF16, BF16, BF16, F32, F32, F32, BF16, BF16)
    return pl.pallas_call(
        _inproj_kernel,
        grid=(seq // tm,),
        in_specs=[pl.BlockSpec((tm, D_MODEL), row),
                  pl.BlockSpec((1, D_MODEL), fixed),
                  pl.BlockSpec((D_MODEL, IN_COLS), fixed, pipeline_mode=pl.Buffered(1)),
                  pl.BlockSpec((tm, LANES), row),
                  pl.BlockSpec((tm, LANES), row)],
        out_specs=[pl.BlockSpec((A_WIDTH, tm), lambda i: (0, i)),
                   pl.BlockSpec((tm, A_WIDTH), row),
                   pl.BlockSpec((A_HEADS, None, dv, tm), lambda i: (0, i // per_kv, 0, i % per_kv))]
                  + [pl.BlockSpec((tm, w), row) for w in widths[3:]],
        out_shape=[jax.ShapeDtypeStruct((A_WIDTH, seq), BF16),
                   jax.ShapeDtypeStruct((seq, A_WIDTH), BF16),
                   jax.ShapeDtypeStruct((A_HEADS, seq // tk, dv, tk), BF16)]
                  + [jax.ShapeDtypeStruct((seq, w), d) for w, d in zip(widths[3:], dtypes[3:])],
        compiler_params=pltpu.CompilerParams(dimension_semantics=("arbitrary",),
                                             vmem_limit_bytes=VMEM_LIMIT),
        name="inproj",
    )(x2, norm_g.reshape(1, D_MODEL), w_bf16, cos_t, sin_t)


LOG2E = math.log2(math.e)
A_ONES_ROWS = 16
A_QCHUNK = 256


def _attn_a_kernel(qt_ref, k_ref, vt_ref, za_ref, g_ref, lq1_ref, lk1_ref, lq2_ref, lk2_ref,
                   o_ref, qz_scr, m_scr, acc_scr, s_a, s_b, p_a, p_b, al_a, al_b,
                   *, tq, tk, lam_init):
    assert tq == tk
    i = pl.program_id(1)
    d = A_HEAD_DIM
    qt = qt_ref[...]
    row = lax.broadcasted_iota(jnp.int32, qt.shape, 0)
    zero = jnp.zeros_like(qt)
    qz_scr[0] = jnp.where(row < d, qt, zero)
    qz_scr[1] = jnp.where(row >= d, qt, zero)
    m_scr[...] = jnp.full(m_scr.shape, -jnp.inf, F32)
    acc_scr[...] = jnp.zeros(acc_scr.shape, F32)

    chunks = [(m, slice(h * A_QCHUNK, (h + 1) * A_QCHUNK)) for h in range(tq // A_QCHUNK) for m in range(2)]

    def scores(j, s_buf, c):
        m, qs = c
        kb = k_ref[pl.ds(pl.multiple_of(j * tk, tk), tk), :]
        s_buf[m, :, qs] = jnp.dot(kb, qz_scr[m, :, qs], preferred_element_type=F32)

    def softmax(s_buf, p_buf, al_buf, masked, c):
        m, qs = c
        s = s_buf[m, :, qs]
        if masked:
            keyi = lax.broadcasted_iota(jnp.int32, s.shape, 0)
            qryi = lax.broadcasted_iota(jnp.int32, s.shape, 1) + qs.start
            s = jnp.where(keyi <= qryi, s, -jnp.inf)
        m_old = m_scr[m, :, qs]
        m_new = jnp.maximum(m_old, jnp.max(s, axis=0, keepdims=True))
        p_buf[m, :, qs] = jnp.exp2(s - m_new).astype(BF16)
        al_buf[m, :, qs] = jnp.exp2(m_old - m_new)
        m_scr[m, :, qs] = m_new

    def values(j, p_buf, al_buf, c):
        m, qs = c
        vtb = vt_ref[j]
        acc_scr[m, :, qs] = (al_buf[m, :, qs] * acc_scr[m, :, qs]
                             + jnp.dot(vtb, p_buf[m, :, qs], preferred_element_type=F32))

    def stage(val=None, sco=None, sm=None):
        for c in chunks:
            if sco is not None:
                scores(*sco, c)
            if val is not None:
                values(*val, c)
            if sm is not None:
                softmax(*sm, c)

    A = (p_a, al_a)
    B = (p_b, al_b)

    stage(sco=(0, s_a))

    @pl.when(i >= 1)
    def _():
        stage(sco=(1, s_b), sm=(s_a, *A, False))

    def pair(t):
        stage(sco=(t, s_a), val=(t - 2, *A), sm=(s_b, *B, False))
        stage(sco=(t + 1, s_b), val=(t - 1, *B), sm=(s_a, *A, False))

    def four_pairs(u, carry):
        for k in range(4):
            pair(2 + 8 * u + 2 * k)
        return carry

    n_pairs = jnp.maximum(i - 1, 0) // 2
    n_quads = n_pairs // 4
    lax.fori_loop(0, n_quads, four_pairs, 0)

    def one_pair(u, carry):
        pair(2 + 8 * n_quads + 2 * u)
        return carry

    lax.fori_loop(0, n_pairs % 4, one_pair, 0)

    @pl.when(i % 2 == 0)
    def _():
        @pl.when(i >= 2)
        def _():
            stage(sco=(i, s_a), val=(i - 2, *A), sm=(s_b, *B, False))
            stage(val=(i - 1, *B))
        stage(sm=(s_a, *A, True))
        stage(val=(i, *A))

    @pl.when(i % 2 == 1)
    def _():
        stage(val=(i - 1, *A))
        stage(sm=(s_b, *B, True))
        stage(val=(i, *B))

    lam = (jnp.exp(jnp.sum(lq1_ref[...] * lk1_ref[...], axis=-1, keepdims=True))
           - jnp.exp(jnp.sum(lq2_ref[...] * lk2_ref[...], axis=-1, keepdims=True)) + lam_init)
    dv = 2 * A_HEAD_DIM
    o0 = acc_scr[0, 0:dv, :] * (1.0 / acc_scr[0, dv:dv + 1, :])
    o1 = acc_scr[1, 0:dv, :] * (1.0 / acc_scr[1, dv:dv + 1, :])
    o = o0 - lam * o1
    y = o * lax.rsqrt(jnp.mean(o * o, axis=0, keepdims=True) + EPS) * g_ref[...]
    y = (y * (1.0 - lam_init)).T
    o_ref[...] = (y * za_ref[...].astype(F32)).astype(o_ref.dtype)


def _attn_a(qt, ka, vt, za, subln_g, lq1, lk1, lq2, lk2, lam_init, tq=512):
    seq = ka.shape[0]
    nk, tk = vt.shape[1], vt.shape[3]
    fixed = lambda h, i: (0, 0)
    vec = lambda a: a.reshape(1, -1)
    return pl.pallas_call(
        functools.partial(_attn_a_kernel, tq=tq, tk=tk, lam_init=lam_init),
        grid=(A_HEADS, seq // tq),
        in_specs=[pl.BlockSpec((LANES, tq), lambda h, i: (h, i)),
                  pl.BlockSpec((seq, LANES), lambda h, i: (0, h)),
                  pl.BlockSpec((None, nk, LANES + A_ONES_ROWS, tk), lambda h, i: (h, 0, 0, 0)),
                  pl.BlockSpec((tq, LANES), lambda h, i: (i, h)),
                  pl.BlockSpec((2 * A_HEAD_DIM, 1), fixed),
                  pl.BlockSpec((1, A_HEAD_DIM), fixed),
                  pl.BlockSpec((1, A_HEAD_DIM), fixed),
                  pl.BlockSpec((1, A_HEAD_DIM), fixed),
                  pl.BlockSpec((1, A_HEAD_DIM), fixed)],
        out_specs=pl.BlockSpec((tq, LANES), lambda h, i: (i, h)),
        out_shape=jax.ShapeDtypeStruct((seq, A_WIDTH), BF16),
        scratch_shapes=[pltpu.VMEM((2, LANES, tq), BF16),
                        pltpu.VMEM((2, 1, tq), F32),
                        pltpu.VMEM((2, 2 * A_HEAD_DIM + A_ONES_ROWS, tq), F32),
                        pltpu.VMEM((2, tk, tq), F32), pltpu.VMEM((2, tk, tq), F32),
                        pltpu.VMEM((2, tk, tq), BF16), pltpu.VMEM((2, tk, tq), BF16),
                        pltpu.VMEM((2, 1, tq), F32), pltpu.VMEM((2, 1, tq), F32)],
        compiler_params=pltpu.CompilerParams(dimension_semantics=("arbitrary", "arbitrary"),
                                             vmem_limit_bytes=VMEM_LIMIT),
        name="attn_a",
    )(qt, ka, vt, za, subln_g.reshape(-1, 1), vec(lq1), vec(lk1), vec(lq2), vec(lk2))


C_TILE = C_BLOCK * max(dil for _, dil in C_PATTERNS)
C_UNROLL = 8


def _attn_c_kernel(q_ref, kp_ref, kc_ref, vp_ref, vc_ref, z_ref, bias_ref, o_ref,
                   kk_scr, vv_scr, out_scr, lse_scr):
    t_idx = pl.program_id(1)
    kk_scr[0:C_TILE, :] = kp_ref[...]
    kk_scr[C_TILE:2 * C_TILE, :] = kc_ref[...]
    vv_scr[0:C_TILE, :] = vp_ref[...]
    vv_scr[C_TILE:2 * C_TILE, :] = vc_ref[...]

    head0 = lax.broadcasted_iota(jnp.int32, (LANES, C_BLOCK), 0) < C_HEAD_DIM

    def pick(t):
        return jnp.where(head0, t[:, :C_BLOCK], t[:, C_BLOCK:])

    for p, (window, dil) in enumerate(C_PATTERNS):
        assert window // dil == C_BLOCK
        span = C_BLOCK * dil

        def rows(b, dil=dil, span=span):
            u = b // dil
            r = b % dil
            q_start = u * span + r
            k_start = C_TILE + (u - 1) * span + r
            first = jnp.logical_and(u == 0, t_idx == 0).astype(jnp.int32)
            if dil == 1:
                return pl.ds(q_start, C_BLOCK), pl.ds(k_start, 2 * C_BLOCK), first
            return (pl.ds(q_start, C_BLOCK, stride=dil), pl.ds(k_start, 2 * C_BLOCK, stride=dil), first)

        def blocks(bb, carry, p=p, rows=rows):
            idx = [rows(bb * C_UNROLL + k) for k in range(C_UNROLL)]
            q_both, v_t, s, e, stats, o_all = [], [], [], [], [], []
            for q_rows, k_rows, _ in idx:
                qt = q_ref[q_rows, :].T
                zero = jnp.zeros_like(qt)
                q_both.append(jnp.concatenate([jnp.where(head0, qt, zero), jnp.where(head0, zero, qt)],
                                              axis=1).astype(BF16))
                v_t.append(vv_scr[k_rows, :].T.astype(BF16))
            for k, (_, k_rows, first) in enumerate(idx):
                sk = jnp.dot(kk_scr[k_rows, :].astype(BF16), q_both[k], preferred_element_type=F32)
                s.append(sk + bias_ref[first])
            for k in range(C_UNROLL):
                mx = jnp.max(s[k], axis=0, keepdims=True)
                ek = jnp.exp2(s[k] - mx)
                den = jnp.sum(ek, axis=0, keepdims=True)
                e.append(ek.astype(BF16))
                stats.append((1.0 / den, mx + jnp.log2(den)))
            for k in range(C_UNROLL):
                o_all.append(jnp.dot(v_t[k], e[k], preferred_element_type=F32))
            wide = (LANES, 2 * C_BLOCK)
            for k, (q_rows, _, _) in enumerate(idx):
                inv, lse = stats[k]
                o_t = pick(o_all[k]) * pick(jnp.broadcast_to(inv, wide))
                l_t = pick(jnp.broadcast_to(lse, wide))
                out_scr[p, q_rows, :] = o_t.T
                lse_scr[p, q_rows, :] = l_t.T
            return carry

        lax.fori_loop(0, C_TILE // C_BLOCK // C_UNROLL, blocks, 0)

    l0, l1, l2 = lse_scr[0], lse_scr[1], lse_scr[2]
    mx = jnp.maximum(jnp.maximum(l0, l1), l2)
    w0 = jnp.exp2(l0 - mx)
    w1 = jnp.exp2(l1 - mx)
    w2 = jnp.exp2(l2 - mx)
    oc = (w0 * out_scr[0] + w1 * out_scr[1] + w2 * out_scr[2]) / (w0 + w1 + w2)
    o_ref[...] = (oc * z_ref[...].astype(F32)).astype(o_ref.dtype)


def _attn_c_bias():
    ki = np.arange(2 * C_BLOCK)[:, None]
    qi = np.tile(np.arange(C_BLOCK), LANES // C_HEAD_DIM)[None, :]
    rel = qi + C_BLOCK - ki
    band = (rel >= 0) & (rel <= C_BLOCK)
    masks = np.stack([band, band & (ki >= C_BLOCK)])
    return jnp.asarray(np.where(masks, 0.0, -np.inf), dtype=F32)


def _attn_c(qc, kc, vc, zc):
    seq = qc.shape[0]
    cur = lambda c, t: (t, c)
    prev = lambda c, t: (jnp.maximum(t - 1, 0), c)
    tile = (C_TILE, LANES)
    return pl.pallas_call(
        _attn_c_kernel,
        grid=(C_WIDTH // LANES, seq // C_TILE),
        in_specs=[pl.BlockSpec(tile, cur),
                  pl.BlockSpec(tile, prev), pl.BlockSpec(tile, cur),
                  pl.BlockSpec(tile, prev), pl.BlockSpec(tile, cur),
                  pl.BlockSpec(tile, cur),
                  pl.BlockSpec((2, 2 * C_BLOCK, 2 * C_BLOCK), lambda c, t: (0, 0, 0))],
        out_specs=pl.BlockSpec(tile, cur),
        out_shape=jax.ShapeDtypeStruct((seq, C_WIDTH), BF16),
        scratch_shapes=[pltpu.VMEM((2 * C_TILE, LANES), F32),
                        pltpu.VMEM((2 * C_TILE, LANES), F32),
                        pltpu.VMEM((len(C_PATTERNS), C_TILE, LANES), F32),
                        pltpu.VMEM((len(C_PATTERNS), C_TILE, LANES), F32)],
        compiler_params=pltpu.CompilerParams(dimension_semantics=("arbitrary", "arbitrary"),
                                             vmem_limit_bytes=VMEM_LIMIT),
        name="attn_c",
    )(qc, kc, kc, vc, vc, zc, _attn_c_bias())


def _merge_kernel(x_ref, ya_ref, yc_ref, guv_ref, zb_ref, gate_ref, lng_ref, lnb_ref,
                  sw_ref, sb_ref, wb_ref, wo_ref, fg_ref, o_ref, *, final_norm):
    tm = x_ref.shape[0]
    guv = guv_ref[...]
    u = guv[:, :B_WIDTH].astype(F32)
    vb = guv[:, B_WIDTH:].astype(F32)
    mu = jnp.mean(vb, axis=-1, keepdims=True)
    var = jnp.mean(jnp.square(vb - mu), axis=-1, keepdims=True)
    vb = ((vb - mu) * lax.rsqrt(var + EPS) * lng_ref[...] + lnb_ref[...]).astype(BF16)
    ti = lax.broadcasted_iota(jnp.int32, (B_CHUNK, B_CHUNK), 0)
    si = lax.broadcasted_iota(jnp.int32, (B_CHUNK, B_CHUNK), 1)
    causal = si <= ti
    rows = []
    for c in range(tm // B_CHUNK):
        cols = []
        for g in range(B_GROUPS):
            w = jnp.where(causal, sw_ref[g], 0.0).astype(BF16)
            blk = vb[c * B_CHUNK:(c + 1) * B_CHUNK, g * B_GROUP_DIM:(g + 1) * B_GROUP_DIM]
            mixed = jnp.dot(w, blk, preferred_element_type=F32) + sb_ref[:, g:g + 1]
            cols.append(mixed)
        rows.append(jnp.concatenate(cols, axis=-1))
    mixed = jnp.concatenate(rows, axis=0)
    yb = (u * mixed * zb_ref[...].astype(F32)).astype(BF16)

    gates = gate_ref[...]
    merged = jnp.zeros((tm, D_MODEL), F32)
    for n, y in enumerate((ya_ref[...], yb, yc_ref[...])):
        pb = jnp.dot(y, wb_ref[n], preferred_element_type=F32)
        merged = merged + gates[:, n * D_MODEL:(n + 1) * D_MODEL].astype(F32) * pb
    out = x_ref[...] + jnp.dot(merged.astype(BF16), wo_ref[...], preferred_element_type=F32)
    if final_norm:
        out = out * lax.rsqrt(jnp.mean(out * out, axis=-1, keepdims=True) + EPS) * fg_ref[...]
    o_ref[...] = out


def _merge(x2, ya, yc, guv, zb, gates, ln_g, ln_b, sgu_w, sgu_b, wb_bf16, wo_bf16, final_g,
           final_norm, tm=256):
    seq = x2.shape[0]
    row = lambda i: (i, 0)
    fixed2 = lambda i: (0, 0)
    fixed3 = lambda i: (0, 0, 0)
    return pl.pallas_call(
        functools.partial(_merge_kernel, final_norm=final_norm),
        grid=(seq // tm,),
        in_specs=[pl.BlockSpec((tm, D_MODEL), row),
                  pl.BlockSpec((tm, A_WIDTH), row),
                  pl.BlockSpec((tm, C_WIDTH), row),
                  pl.BlockSpec((tm, 2 * B_WIDTH), row),
                  pl.BlockSpec((tm, B_WIDTH), row),
                  pl.BlockSpec((tm, N_BRANCH * D_MODEL), row),
                  pl.BlockSpec((1, B_WIDTH), fixed2),
                  pl.BlockSpec((1, B_WIDTH), fixed2),
                  pl.BlockSpec((B_GROUPS, B_CHUNK, B_CHUNK), fixed3),
                  pl.BlockSpec((B_CHUNK, B_GROUPS), fixed2),
                  pl.BlockSpec((N_BRANCH, BRANCH_WIDTH, D_MODEL), fixed3),
                  pl.BlockSpec((D_MODEL, D_MODEL), fixed2),
                  pl.BlockSpec((1, D_MODEL), fixed2)],
        out_specs=pl.BlockSpec((tm, D_MODEL), row),
        out_shape=jax.ShapeDtypeStruct((seq, D_MODEL), F32),
        compiler_params=pltpu.CompilerParams(dimension_semantics=("arbitrary",),
                                             vmem_limit_bytes=VMEM_LIMIT),
        name="merge",
    )(x2, ya, yc, guv, zb, gates, ln_g.reshape(1, -1), ln_b.reshape(1, -1), sgu_w, sgu_b.T,
      wb_bf16, wo_bf16, final_g.reshape(1, -1))


def kernel(x, positions, norm_g, w_in, lam_q1, lam_k1, lam_q2, lam_k2, subln_g, sgu_ln_g, sgu_ln_b,
           sgu_w, sgu_b, w_branch, w_out, final_g):
    batch, seq, _ = x.shape
    depth = norm_g.shape[0]
    assert batch == 1 and seq % C_TILE == 0
    h = x.reshape(seq, D_MODEL)
    cos_t, sin_t = _rope_tables(positions.reshape(seq), seq)
    for l in range(depth):
        lam_init = 0.8 - 0.6 * math.exp(-0.3 * l)
        (qt, ka, vt, za, guv, zb, qc, kc, vc, zc, gates) = _inproj(
            h, norm_g[l], w_in[l].astype(BF16), cos_t, sin_t)
        ya = _attn_a(qt, ka, vt, za, subln_g[l], lam_q1[l], lam_k1[l], lam_q2[l], lam_k2[l], lam_init)
        yc = _attn_c(qc, kc, vc, zc)
        h = _merge(h, ya, yc, guv, zb, gates, sgu_ln_g[l], sgu_ln_b[l], sgu_w[l], sgu_b[l],
                   w_branch[l].astype(BF16), w_out[l].astype(BF16), final_g,
                   final_norm=(l == depth - 1))
    return h.reshape(batch, seq, D_MODEL)
```

```python
import functools
import math

import jax
import jax.numpy as jnp
import numpy as np
from jax import lax
from jax.experimental import pallas as pl
from jax.experimental.pallas import tpu as pltpu

D_MODEL = 1024
A_HEADS = 4
A_HEAD_DIM = 64
A_WIDTH = A_HEADS * 2 * A_HEAD_DIM
B_WIDTH = 512
B_GROUPS = 4
B_GROUP_DIM = B_WIDTH // B_GROUPS
B_CHUNK = 128
C_HEADS = 8
C_HEAD_DIM = 64
C_WIDTH = C_HEADS * C_HEAD_DIM
C_PATTERNS = ((128, 1), (512, 4), (2048, 16))
C_BLOCK = 128
N_BRANCH = 3
BRANCH_WIDTH = 512
ROPE_THETA = 500000.0
ROPE_FRAC = 4
EPS = 1e-6

LANES = 128
VMEM_LIMIT = 56 * 1024 * 1024

_SIZES = (A_WIDTH, A_WIDTH, A_WIDTH, A_WIDTH, 2 * B_WIDTH, B_WIDTH,
          C_WIDTH, C_WIDTH, C_WIDTH, C_WIDTH, N_BRANCH * D_MODEL)
_OFFS = tuple(int(sum(_SIZES[:i])) for i in range(len(_SIZES)))
IN_COLS = int(sum(_SIZES))

F32 = jnp.float32
BF16 = jnp.bfloat16


def _silu(t):
    return t * (1.0 / (1.0 + jnp.exp(-t)))


def _sigmoid(t):
    return 1.0 / (1.0 + jnp.exp(-t))


def _gelu_tanh(t):
    c = math.sqrt(2.0 / math.pi)
    return 0.5 * t * (1.0 + jnp.tanh(c * (t + 0.044715 * (t * t * t))))


def _rope_table_kernel(pos_ref, inv_ref, sgn_ref, cos_ref, sin_ref):
    ang = pos_ref[...].astype(F32) * inv_ref[...]
    cos_ref[...] = jnp.cos(ang)
    sin_ref[...] = jnp.sin(ang) * sgn_ref[...]


def _rope_tables(positions, seq):
    rot = A_HEAD_DIM // ROPE_FRAC
    half = rot // 2
    inv = jnp.power(jnp.float32(ROPE_THETA), -jnp.arange(half, dtype=jnp.float32) * 2.0 / rot)
    zeros = jnp.zeros((A_HEAD_DIM - rot,), F32)
    inv_head = jnp.concatenate([inv, inv, zeros])
    sgn_head = jnp.concatenate([-jnp.ones((half,), F32), jnp.ones((half,), F32), zeros])
    reps = LANES // A_HEAD_DIM
    inv_lane = jnp.tile(inv_head, reps).reshape(1, LANES)
    sgn_lane = jnp.tile(sgn_head, reps).reshape(1, LANES)
    tm = 1024
    return pl.pallas_call(
        _rope_table_kernel,
        grid=(seq // tm,),
        in_specs=[pl.BlockSpec((tm, 1), lambda i: (i, 0)),
                  pl.BlockSpec((1, LANES), lambda i: (0, 0)),
                  pl.BlockSpec((1, LANES), lambda i: (0, 0))],
        out_specs=[pl.BlockSpec((tm, LANES), lambda i: (i, 0)),
                   pl.BlockSpec((tm, LANES), lambda i: (i, 0))],
        out_shape=[jax.ShapeDtypeStruct((seq, LANES), F32)] * 2,
        name="rope_tables",
    )(positions.reshape(seq, 1), inv_lane, sgn_lane)


def _inproj_kernel(x_ref, g_ref, w_ref, cos_ref, sin_ref,
                   qt_ref, ka_ref, vt_ref, za_ref, guv_ref, zb_ref,
                   qc_ref, kc_ref, vc_ref, zc_ref, gate_ref):
    x = x_ref[...]
    h = x * lax.rsqrt(jnp.mean(x * x, axis=-1, keepdims=True) + EPS) * g_ref[...]
    h = h.astype(BF16)
    cos = cos_ref[...]
    sin = sin_ref[...]
    tm = x.shape[0]
    lane = lax.broadcasted_iota(jnp.int32, (tm, LANES), 1)
    first_half = (lane % A_HEAD_DIM) < (A_HEAD_DIM // ROPE_FRAC // 2)
    shift = A_HEAD_DIM // ROPE_FRAC // 2

    def proj(lo, n):
        return jnp.dot(h, w_ref[:, lo:lo + n], preferred_element_type=F32)

    def rope_store(seg, out_ref, scale, width=A_WIDTH, transposed=False):
        full = proj(_OFFS[seg], width)
        for c in range(width // LANES):
            t = full[:, c * LANES:(c + 1) * LANES]
            partner = jnp.where(first_half, pltpu.roll(t, LANES - shift, 1), pltpu.roll(t, shift, 1))
            r = t * cos + partner * sin
            if scale != 1.0:
                r = r * scale
            if transposed:
                out_ref[c * LANES:(c + 1) * LANES, :] = r.T.astype(out_ref.dtype)
            else:
                out_ref[:, c * LANES:(c + 1) * LANES] = r.astype(out_ref.dtype)

    def act_store(seg, out_ref, fn, width=512):
        n = out_ref.shape[1]
        for c in range(n // width):
            t = proj(_OFFS[seg] + c * width, width)
            out_ref[:, c * width:(c + 1) * width] = fn(t).astype(out_ref.dtype)

    ident = lambda t: t
    rope_store(0, qt_ref, LOG2E / math.sqrt(A_HEAD_DIM), transposed=True)
    rope_store(1, ka_ref, 1.0)
    va = proj(_OFFS[2], A_WIDTH)
    dv = 2 * A_HEAD_DIM
    for hd in range(A_HEADS):
        vt_ref[hd, 0:dv, :] = va[:, hd * dv:(hd + 1) * dv].T.astype(vt_ref.dtype)
        vt_ref[hd, dv:dv + A_ONES_ROWS, :] = jnp.ones((A_ONES_ROWS, tm), vt_ref.dtype)
    act_store(3, za_ref, _silu)
    act_store(4, guv_ref, _gelu_tanh)
    act_store(5, zb_ref, _silu)
    rope_store(6, qc_ref, LOG2E / math.sqrt(C_HEAD_DIM))
    rope_store(7, kc_ref, 1.0)
    act_store(8, vc_ref, ident)
    act_store(9, zc_ref, _silu)
    act_store(10, gate_ref, _sigmoid)


def _inproj(x2, norm_g, w_bf16, cos_t, sin_t, tm=256, tk=512):
    seq = x2.shape[0]
    row = lambda i: (i, 0)
    fixed = lambda i: (0, 0)
    per_kv = tk // tm
    dv = 2 * A_HEAD_DIM + A_ONES_ROWS
    widths = (A_WIDTH, A_WIDTH, A_WIDTH, A_WIDTH, 2 * B_WIDTH, B_WIDTH,
              C_WIDTH, C_WIDTH, C_WIDTH, C_WIDTH, N_BRANCH * D_MODEL)
    dtypes = (BF16, BF16, BF16, BF16, BF16, BF16, F32, F32, F32, BF16, BF16)
    return pl.pallas_call(
        _inproj_kernel,
        grid=(seq // tm,),
        in_specs=[pl.BlockSpec((tm, D_MODEL), row),
                  pl.BlockSpec((1, D_MODEL), fixed),
                  pl.BlockSpec((D_MODEL, IN_COLS), fixed, pipeline_mode=pl.Buffered(1)),
                  pl.BlockSpec((tm, LANES), row),
                  pl.BlockSpec((tm, LANES), row)],
        out_specs=[pl.BlockSpec((A_WIDTH, tm), lambda i: (0, i)),
                   pl.BlockSpec((tm, A_WIDTH), row),
                   pl.BlockSpec((A_HEADS, None, dv, tm), lambda i: (0, i // per_kv, 0, i % per_kv))]
                  + [pl.BlockSpec((tm, w), row) for w in widths[3:]],
        out_shape=[jax.ShapeDtypeStruct((A_WIDTH, seq), BF16),
                   jax.ShapeDtypeStruct((seq, A_WIDTH), BF16),
                   jax.ShapeDtypeStruct((A_HEADS, seq // tk, dv, tk), BF16)]
                  + [jax.ShapeDtypeStruct((seq, w), d) for w, d in zip(widths[3:], dtypes[3:])],
        compiler_params=pltpu.CompilerParams(dimension_semantics=("arbitrary",),
                                             vmem_limit_bytes=VMEM_LIMIT),
        name="inproj",
    )(x2, norm_g.reshape(1, D_MODEL), w_bf16, cos_t, sin_t)


LOG2E = math.log2(math.e)
A_ONES_ROWS = 16
A_QCHUNK = 256


def _attn_a_kernel(qt_ref, k_ref, vt_ref, za_ref, g_ref, lq1_ref, lk1_ref, lq2_ref, lk2_ref,
                   o_ref, qz_scr, m_scr, acc_scr, s_a, s_b, p_a, p_b, al_a, al_b,
                   *, tq, tk, lam_init):
    assert tq == tk
    i = pl.program_id(1)
    d = A_HEAD_DIM
    qt = qt_ref[...]
    row = lax.broadcasted_iota(jnp.int32, qt.shape, 0)
    zero = jnp.zeros_like(qt)
    qz_scr[0] = jnp.where(row < d, qt, zero)
    qz_scr[1] = jnp.where(row >= d, qt, zero)
    m_scr[...] = jnp.full(m_scr.shape, -jnp.inf, F32)
    acc_scr[...] = jnp.zeros(acc_scr.shape, F32)

    chunks = [(m, slice(h * A_QCHUNK, (h + 1) * A_QCHUNK)) for h in range(tq // A_QCHUNK) for m in range(2)]

    def scores(j, s_buf, c):
        m, qs = c
        kb = k_ref[pl.ds(pl.multiple_of(j * tk, tk), tk), :]
        s_buf[m, :, qs] = jnp.dot(kb, qz_scr[m, :, qs], preferred_element_type=F32)

    def softmax(s_buf, p_buf, al_buf, masked, c):
        m, qs = c
        s = s_buf[m, :, qs]
        if masked:
            keyi = lax.broadcasted_iota(jnp.int32, s.shape, 0)
            qryi = lax.broadcasted_iota(jnp.int32, s.shape, 1) + qs.start
            s = jnp.where(keyi <= qryi, s, -jnp.inf)
        m_old = m_scr[m, :, qs]
        m_new = jnp.maximum(m_old, jnp.max(s, axis=0, keepdims=True))
        p_buf[m, :, qs] = jnp.exp2(s - m_new).astype(BF16)
        al_buf[m, :, qs] = jnp.exp2(m_old - m_new)
        m_scr[m, :, qs] = m_new

    def values(j, p_buf, al_buf, c):
        m, qs = c
        vtb = vt_ref[j]
        acc_scr[m, :, qs] = (al_buf[m, :, qs] * acc_scr[m, :, qs]
                             + jnp.dot(vtb, p_buf[m, :, qs], preferred_element_type=F32))

    def stage(val=None, sco=None, sm=None):
        for c in chunks:
            if sco is not None:
                scores(*sco, c)
            if val is not None:
                values(*val, c)
            if sm is not None:
                softmax(*sm, c)

    A = (p_a, al_a)
    B = (p_b, al_b)

    stage(sco=(0, s_a))

    @pl.when(i >= 1)
    def _():
        stage(sco=(1, s_b), sm=(s_a, *A, False))

    def pair(t):
        stage(sco=(t, s_a), val=(t - 2, *A), sm=(s_b, *B, False))
        stage(sco=(t + 1, s_b), val=(t - 1, *B), sm=(s_a, *A, False))

    def four_pairs(u, carry):
        for k in range(4):
            pair(2 + 8 * u + 2 * k)
        return carry

    n_pairs = jnp.maximum(i - 1, 0) // 2
    n_quads = n_pairs // 4
    lax.fori_loop(0, n_quads, four_pairs, 0)

    def one_pair(u, carry):
        pair(2 + 8 * n_quads + 2 * u)
        return carry

    lax.fori_loop(0, n_pairs % 4, one_pair, 0)

    @pl.when(i % 2 == 0)
    def _():
        @pl.when(i >= 2)
        def _():
            stage(sco=(i, s_a), val=(i - 2, *A), sm=(s_b, *B, False))
            stage(val=(i - 1, *B))
        stage(sm=(s_a, *A, True))
        stage(val=(i, *A))

    @pl.when(i % 2 == 1)
    def _():
        stage(val=(i - 1, *A))
        stage(sm=(s_b, *B, True))
        stage(val=(i, *B))

    lam = (jnp.exp(jnp.sum(lq1_ref[...] * lk1_ref[...], axis=-1, keepdims=True))
           - jnp.exp(jnp.sum(lq2_ref[...] * lk2_ref[...], axis=-1, keepdims=True)) + lam_init)
    dv = 2 * A_HEAD_DIM
    o0 = acc_scr[0, 0:dv, :] * (1.0 / acc_scr[0, dv:dv + 1, :])
    o1 = acc_scr[1, 0:dv, :] * (1.0 / acc_scr[1, dv:dv + 1, :])
    o = o0 - lam * o1
    y = o * lax.rsqrt(jnp.mean(o * o, axis=0, keepdims=True) + EPS) * g_ref[...]
    y = (y * (1.0 - lam_init)).T
    o_ref[...] = (y * za_ref[...].astype(F32)).astype(o_ref.dtype)


def _attn_a(qt, ka, vt, za, subln_g, lq1, lk1, lq2, lk2, lam_init, tq=512):
    seq = ka.shape[0]
    nk, tk = vt.shape[1], vt.shape[3]
    fixed = lambda h, i: (0, 0)
    vec = lambda a: a.reshape(1, -1)
    return pl.pallas_call(
        functools.partial(_attn_a_kernel, tq=tq, tk=tk, lam_init=lam_init),
        grid=(A_HEADS, seq // tq),
        in_specs=[pl.BlockSpec((LANES, tq), lambda h, i: (h, i)),
                  pl.BlockSpec((seq, LANES), lambda h, i: (0, h)),
                  pl.BlockSpec((None, nk, LANES + A_ONES_ROWS, tk), lambda h, i: (h, 0, 0, 0)),
                  pl.BlockSpec((tq, LANES), lambda h, i: (i, h)),
                  pl.BlockSpec((2 * A_HEAD_DIM, 1), fixed),
                  pl.BlockSpec((1, A_HEAD_DIM), fixed),
                  pl.BlockSpec((1, A_HEAD_DIM), fixed),
                  pl.BlockSpec((1, A_HEAD_DIM), fixed),
                  pl.BlockSpec((1, A_HEAD_DIM), fixed)],
        out_specs=pl.BlockSpec((tq, LANES), lambda h, i: (i, h)),
        out_shape=jax.ShapeDtypeStruct((seq, A_WIDTH), BF16),
        scratch_shapes=[pltpu.VMEM((2, LANES, tq), BF16),
                        pltpu.VMEM((2, 1, tq), F32),
                        pltpu.VMEM((2, 2 * A_HEAD_DIM + A_ONES_ROWS, tq), F32),
                        pltpu.VMEM((2, tk, tq), F32), pltpu.VMEM((2, tk, tq), F32),
                        pltpu.VMEM((2, tk, tq), BF16), pltpu.VMEM((2, tk, tq), BF16),
                        pltpu.VMEM((2, 1, tq), F32), pltpu.VMEM((2, 1, tq), F32)],
        compiler_params=pltpu.CompilerParams(dimension_semantics=("arbitrary", "arbitrary"),
                                             vmem_limit_bytes=VMEM_LIMIT),
        name="attn_a",
    )(qt, ka, vt, za, subln_g.reshape(-1, 1), vec(lq1), vec(lk1), vec(lq2), vec(lk2))


C_TILE = C_BLOCK * max(dil for _, dil in C_PATTERNS)


def _attn_c_kernel(q_ref, kp_ref, kc_ref, vp_ref, vc_ref, z_ref, bias_ref, o_ref, out_scr, lse_scr):
    first_tile = (pl.program_id(1) == 0).astype(jnp.int32)
    head0 = lax.broadcasted_iota(jnp.int32, (LANES, C_BLOCK), 0) < C_HEAD_DIM

    def pick(t):
        return jnp.where(head0, t[:, :C_BLOCK], t[:, C_BLOCK:])

    def strided(start, dil):
        return pl.ds(start, C_BLOCK) if dil == 1 else pl.ds(start, C_BLOCK, stride=dil)

    for p, (window, dil) in enumerate(C_PATTERNS):
        assert window // dil == C_BLOCK
        span = C_BLOCK * dil
        n_blocks = C_TILE // C_BLOCK
        idx = []
        for b in range(n_blocks):
            u, r = divmod(b, dil)
            older = strided(C_TILE - span + r, dil) if u == 0 else strided((u - 1) * span + r, dil)
            idx.append((strided(u * span + r, dil), older, u == 0))

        def keys(prev_ref, cur_ref, q_rows, older, in_prev):
            return jnp.concatenate([(prev_ref if in_prev else cur_ref)[older, :], cur_ref[q_rows, :]], axis=0)

        q_both, v_t, s, e, stats, o_all = [], [], [], [], [], []
        for q_rows, older, in_prev in idx:
            qt = q_ref[q_rows, :].T
            zero = jnp.zeros_like(qt)
            q_both.append(jnp.concatenate([jnp.where(head0, qt, zero), jnp.where(head0, zero, qt)],
                                          axis=1).astype(BF16))
            v_t.append(keys(vp_ref, vc_ref, q_rows, older, in_prev).T.astype(BF16))
        for k, (q_rows, older, in_prev) in enumerate(idx):
            k2 = keys(kp_ref, kc_ref, q_rows, older, in_prev).astype(BF16)
            bias = bias_ref[first_tile] if in_prev else bias_ref[0]
            s.append(jnp.dot(k2, q_both[k], preferred_element_type=F32) + bias)
        for k in range(n_blocks):
            mx = jnp.max(s[k], axis=0, keepdims=True)
            ek = jnp.exp2(s[k] - mx)
            den = jnp.sum(ek, axis=0, keepdims=True)
            e.append(ek.astype(BF16))
            stats.append((1.0 / den, mx + jnp.log2(den)))
        for k in range(n_blocks):
            o_all.append(jnp.dot(v_t[k], e[k], preferred_element_type=F32))
        wide = (LANES, 2 * C_BLOCK)
        for k, (q_rows, _, _) in enumerate(idx):
            inv, lse = stats[k]
            o_t = pick(o_all[k]) * pick(jnp.broadcast_to(inv, wide))
            l_t = pick(jnp.broadcast_to(lse, wide))
            out_scr[p, q_rows, :] = o_t.T
            lse_scr[p, q_rows, :] = l_t.T

    l0, l1, l2 = lse_scr[0], lse_scr[1], lse_scr[2]
    mx = jnp.maximum(jnp.maximum(l0, l1), l2)
    w0 = jnp.exp2(l0 - mx)
    w1 = jnp.exp2(l1 - mx)
    w2 = jnp.exp2(l2 - mx)
    oc = (w0 * out_scr[0] + w1 * out_scr[1] + w2 * out_scr[2]) / (w0 + w1 + w2)
    o_ref[...] = (oc * z_ref[...].astype(F32)).astype(o_ref.dtype)


def _attn_c_bias():
    ki = np.arange(2 * C_BLOCK)[:, None]
    qi = np.tile(np.arange(C_BLOCK), LANES // C_HEAD_DIM)[None, :]
    rel = qi + C_BLOCK - ki
    band = (rel >= 0) & (rel <= C_BLOCK)
    masks = np.stack([band, band & (ki >= C_BLOCK)])
    return jnp.asarray(np.where(masks, 0.0, -np.inf), dtype=F32)


def _attn_c(qc, kc, vc, zc):
    seq = qc.shape[0]
    cur = lambda c, t: (t, c)
    prev = lambda c, t: (jnp.maximum(t - 1, 0), c)
    tile = (C_TILE, LANES)
    return pl.pallas_call(
        _attn_c_kernel,
        grid=(C_WIDTH // LANES, seq // C_TILE),
        in_specs=[pl.BlockSpec(tile, cur),
                  pl.BlockSpec(tile, prev), pl.BlockSpec(tile, cur),
                  pl.BlockSpec(tile, prev), pl.BlockSpec(tile, cur),
                  pl.BlockSpec(tile, cur),
                  pl.BlockSpec((2, 2 * C_BLOCK, 2 * C_BLOCK), lambda c, t: (0, 0, 0))],
        out_specs=pl.BlockSpec(tile, cur),
        out_shape=jax.ShapeDtypeStruct((seq, C_WIDTH), BF16),
        scratch_shapes=[pltpu.VMEM((len(C_PATTERNS), C_TILE, LANES), F32),
                        pltpu.VMEM((len(C_PATTERNS), C_TILE, LANES), F32)],
        compiler_params=pltpu.CompilerParams(dimension_semantics=("arbitrary", "arbitrary"),
                                             vmem_limit_bytes=VMEM_LIMIT),
        name="attn_c",
    )(qc, kc, kc, vc, vc, zc, _attn_c_bias())


def _merge_kernel(x_ref, ya_ref, yc_ref, guv_ref, zb_ref, gate_ref, lng_ref, lnb_ref,
                  sw_ref, sb_ref, wb_ref, wo_ref, fg_ref, o_ref, *, final_norm):
    tm = x_ref.shape[0]
    guv = guv_ref[...]
    u = guv[:, :B_WIDTH].astype(F32)
    vb = guv[:, B_WIDTH:].astype(F32)
    mu = jnp.mean(vb, axis=-1, keepdims=True)
    var = jnp.mean(jnp.square(vb - mu), axis=-1, keepdims=True)
    vb = ((vb - mu) * lax.rsqrt(var + EPS) * lng_ref[...] + lnb_ref[...]).astype(BF16)
    ti = lax.broadcasted_iota(jnp.int32, (B_CHUNK, B_CHUNK), 0)
    si = lax.broadcasted_iota(jnp.int32, (B_CHUNK, B_CHUNK), 1)
    causal = si <= ti
    rows = []
    for c in range(tm // B_CHUNK):
        cols = []
        for g in range(B_GROUPS):
            w = jnp.where(causal, sw_ref[g], 0.0).astype(BF16)
            blk = vb[c * B_CHUNK:(c + 1) * B_CHUNK, g * B_GROUP_DIM:(g + 1) * B_GROUP_DIM]
            mixed = jnp.dot(w, blk, preferred_element_type=F32) + sb_ref[:, g:g + 1]
            cols.append(mixed)
        rows.append(jnp.concatenate(cols, axis=-1))
    mixed = jnp.concatenate(rows, axis=0)
    yb = (u * mixed * zb_ref[...].astype(F32)).astype(BF16)

    gates = gate_ref[...]
    merged = jnp.zeros((tm, D_MODEL), F32)
    for n, y in enumerate((ya_ref[...], yb, yc_ref[...])):
        pb = jnp.dot(y, wb_ref[n], preferred_element_type=F32)
        merged = merged + gates[:, n * D_MODEL:(n + 1) * D_MODEL].astype(F32) * pb
    out = x_ref[...] + jnp.dot(merged.astype(BF16), wo_ref[...], preferred_element_type=F32)
    if final_norm:
        out = out * lax.rsqrt(jnp.mean(out * out, axis=-1, keepdims=True) + EPS) * fg_ref[...]
    o_ref[...] = out


def _merge(x2, ya, yc, guv, zb, gates, ln_g, ln_b, sgu_w, sgu_b, wb_bf16, wo_bf16, final_g,
           final_norm, tm=512):
    seq = x2.shape[0]
    row = lambda i: (i, 0)
    fixed2 = lambda i: (0, 0)
    fixed3 = lambda i: (0, 0, 0)
    return pl.pallas_call(
        functools.partial(_merge_kernel, final_norm=final_norm),
        grid=(seq // tm,),
        in_specs=[pl.BlockSpec((tm, D_MODEL), row),
                  pl.BlockSpec((tm, A_WIDTH), row),
                  pl.BlockSpec((tm, C_WIDTH), row),
                  pl.BlockSpec((tm, 2 * B_WIDTH), row),
                  pl.BlockSpec((tm, B_WIDTH), row),
                  pl.BlockSpec((tm, N_BRANCH * D_MODEL), row),
                  pl.BlockSpec((1, B_WIDTH), fixed2),
                  pl.BlockSpec((1, B_WIDTH), fixed2),
                  pl.BlockSpec((B_GROUPS, B_CHUNK, B_CHUNK), fixed3),
                  pl.BlockSpec((B_CHUNK, B_GROUPS), fixed2),
                  pl.BlockSpec((N_BRANCH, BRANCH_WIDTH, D_MODEL), fixed3),
                  pl.BlockSpec((D_MODEL, D_MODEL), fixed2),
                  pl.BlockSpec((1, D_MODEL), fixed2)],
        out_specs=pl.BlockSpec((tm, D_MODEL), row),
        out_shape=jax.ShapeDtypeStruct((seq, D_MODEL), F32),
        compiler_params=pltpu.CompilerParams(dimension_semantics=("arbitrary",),
                                             vmem_limit_bytes=VMEM_LIMIT),
        name="merge",
    )(x2, ya, yc, guv, zb, gates, ln_g.reshape(1, -1), ln_b.reshape(1, -1), sgu_w, sgu_b.T,
      wb_bf16, wo_bf16, final_g.reshape(1, -1))


def kernel(x, positions, norm_g, w_in, lam_q1, lam_k1, lam_q2, lam_k2, subln_g, sgu_ln_g, sgu_ln_b,
           sgu_w, sgu_b, w_branch, w_out, final_g):
    batch, seq, _ = x.shape
    depth = norm_g.shape[0]
    assert batch == 1 and seq % C_TILE == 0
    h = x.reshape(seq, D_MODEL)
    cos_t, sin_t = _rope_tables(positions.reshape(seq), seq)
    for l in range(depth):
        lam_init = 0.8 - 0.6 * math.exp(-0.3 * l)
        (qt, ka, vt, za, guv, zb, qc, kc, vc, zc, gates) = _inproj(
            h, norm_g[l], w_in[l].astype(BF16), cos_t, sin_t)
        ya = _attn_a(qt, ka, vt, za, subln_g[l], lam_q1[l], lam_k1[l], lam_q2[l], lam_k2[l], lam_init)
        yc = _attn_c(qc, kc, vc, zc)
        h = _merge(h, ya, yc, guv, zb, gates, sgu_ln_g[l], sgu_ln_b[l], sgu_w[l], sgu_b[l],
                   w_branch[l].astype(BF16), w_out[l].astype(BF16), final_g,
                   final_norm=(l == depth - 1))
    return h.reshape(batch, seq, D_MODEL)
```

```python
import functools
import math

import jax
import jax.numpy as jnp
import numpy as np
from jax import lax
from jax.experimental import pallas as pl
from jax.experimental.pallas import tpu as pltpu

D_MODEL = 1024
A_HEADS = 4
A_HEAD_DIM = 64
A_WIDTH = A_HEADS * 2 * A_HEAD_DIM
B_WIDTH = 512
B_GROUPS = 4
B_GROUP_DIM = B_WIDTH // B_GROUPS
B_CHUNK = 128
C_HEADS = 8
C_HEAD_DIM = 64
C_WIDTH = C_HEADS * C_HEAD_DIM
C_PATTERNS = ((128, 1), (512, 4), (2048, 16))
C_BLOCK = 128
N_BRANCH = 3
BRANCH_WIDTH = 512
ROPE_THETA = 500000.0
ROPE_FRAC = 4
EPS = 1e-6

LANES = 128
VMEM_LIMIT = 56 * 1024 * 1024

INPROJ_ROWS = 256
ATTN_A_BLOCK = 512
MERGE_ROWS = 512

_SIZES = (A_WIDTH, A_WIDTH, A_WIDTH, A_WIDTH, 2 * B_WIDTH, B_WIDTH,
          C_WIDTH, C_WIDTH, C_WIDTH, C_WIDTH, N_BRANCH * D_MODEL)
_OFFS = tuple(int(sum(_SIZES[:i])) for i in range(len(_SIZES)))
IN_COLS = int(sum(_SIZES))

F32 = jnp.float32
BF16 = jnp.bfloat16


def _silu(t):
    return t * (1.0 / (1.0 + jnp.exp(-t)))


def _sigmoid(t):
    return 1.0 / (1.0 + jnp.exp(-t))


def _gelu_tanh(t):
    c = math.sqrt(2.0 / math.pi)
    return 0.5 * t * (1.0 + jnp.tanh(c * (t + 0.044715 * (t * t * t))))


def _rope_table_kernel(pos_ref, inv_ref, sgn_ref, cos_ref, sin_ref):
    ang = pos_ref[...].astype(F32) * inv_ref[...]
    cos_ref[...] = jnp.cos(ang)
    sin_ref[...] = jnp.sin(ang) * sgn_ref[...]


def _rope_tables(positions, seq):
    rot = A_HEAD_DIM // ROPE_FRAC
    half = rot // 2
    inv = jnp.power(jnp.float32(ROPE_THETA), -jnp.arange(half, dtype=jnp.float32) * 2.0 / rot)
    zeros = jnp.zeros((A_HEAD_DIM - rot,), F32)
    inv_head = jnp.concatenate([inv, inv, zeros])
    sgn_head = jnp.concatenate([-jnp.ones((half,), F32), jnp.ones((half,), F32), zeros])
    reps = LANES // A_HEAD_DIM
    inv_lane = jnp.tile(inv_head, reps).reshape(1, LANES)
    sgn_lane = jnp.tile(sgn_head, reps).reshape(1, LANES)
    tm = 1024
    return pl.pallas_call(
        _rope_table_kernel,
        grid=(seq // tm,),
        in_specs=[pl.BlockSpec((tm, 1), lambda i: (i, 0)),
                  pl.BlockSpec((1, LANES), lambda i: (0, 0)),
                  pl.BlockSpec((1, LANES), lambda i: (0, 0))],
        out_specs=[pl.BlockSpec((tm, LANES), lambda i: (i, 0)),
                   pl.BlockSpec((tm, LANES), lambda i: (i, 0))],
        out_shape=[jax.ShapeDtypeStruct((seq, LANES), F32)] * 2,
        name="rope_tables",
    )(positions.reshape(seq, 1), inv_lane, sgn_lane)


def _inproj_kernel(x_ref, g_ref, w_ref, cos_ref, sin_ref,
                   qt_ref, ka_ref, vt_ref, za_ref, guv_ref, zb_ref,
                   qc_ref, kc_ref, vc_ref, zc_ref, gate_ref):
    x = x_ref[...]
    h = x * lax.rsqrt(jnp.mean(x * x, axis=-1, keepdims=True) + EPS) * g_ref[...]
    h = h.astype(BF16)
    cos = cos_ref[...]
    sin = sin_ref[...]
    tm = x.shape[0]
    lane = lax.broadcasted_iota(jnp.int32, (tm, LANES), 1)
    first_half = (lane % A_HEAD_DIM) < (A_HEAD_DIM // ROPE_FRAC // 2)
    shift = A_HEAD_DIM // ROPE_FRAC // 2

    def proj(lo, n):
        return jnp.dot(h, w_ref[:, lo:lo + n], preferred_element_type=F32)

    def rope_store(seg, out_ref, scale, width=A_WIDTH, transposed=False):
        full = proj(_OFFS[seg], width)
        for c in range(width // LANES):
            t = full[:, c * LANES:(c + 1) * LANES]
            partner = jnp.where(first_half, pltpu.roll(t, LANES - shift, 1), pltpu.roll(t, shift, 1))
            r = t * cos + partner * sin
            if scale != 1.0:
                r = r * scale
            if transposed:
                out_ref[c * LANES:(c + 1) * LANES, :] = r.T.astype(out_ref.dtype)
            else:
                out_ref[:, c * LANES:(c + 1) * LANES] = r.astype(out_ref.dtype)

    def act_store(seg, out_ref, fn, width=512):
        n = out_ref.shape[1]
        for c in range(n // width):
            t = proj(_OFFS[seg] + c * width, width)
            out_ref[:, c * width:(c + 1) * width] = fn(t).astype(out_ref.dtype)

    ident = lambda t: t
    rope_store(0, qt_ref, LOG2E / math.sqrt(A_HEAD_DIM), transposed=True)
    rope_store(1, ka_ref, 1.0)
    va = proj(_OFFS[2], A_WIDTH)
    dv = 2 * A_HEAD_DIM
    for hd in range(A_HEADS):
        vt_ref[hd, 0:dv, :] = va[:, hd * dv:(hd + 1) * dv].T.astype(vt_ref.dtype)
        vt_ref[hd, dv:dv + A_ONES_ROWS, :] = jnp.ones((A_ONES_ROWS, tm), vt_ref.dtype)
    act_store(3, za_ref, _silu)
    act_store(4, guv_ref, _gelu_tanh)
    act_store(5, zb_ref, _silu)
    rope_store(6, qc_ref, LOG2E / math.sqrt(C_HEAD_DIM))
    rope_store(7, kc_ref, 1.0)
    act_store(8, vc_ref, ident)
    act_store(9, zc_ref, _silu)
    act_store(10, gate_ref, _sigmoid)


def _inproj(x2, norm_g, w_bf16, cos_t, sin_t, tm=INPROJ_ROWS, tk=ATTN_A_BLOCK):
    seq = x2.shape[0]
    row = lambda i: (i, 0)
    fixed = lambda i: (0, 0)
    per_kv = tk // tm
    dv = 2 * A_HEAD_DIM + A_ONES_ROWS
    widths = (A_WIDTH, A_WIDTH, A_WIDTH, A_WIDTH, 2 * B_WIDTH, B_WIDTH,
              C_WIDTH, C_WIDTH, C_WIDTH, C_WIDTH, N_BRANCH * D_MODEL)
    dtypes = (BF16, BF16, BF16, BF16, BF16, BF16, F32, F32, F32, BF16, BF16)
    return pl.pallas_call(
        _inproj_kernel,
        grid=(seq // tm,),
        in_specs=[pl.BlockSpec((tm, D_MODEL), row),
                  pl.BlockSpec((1, D_MODEL), fixed),
                  pl.BlockSpec((D_MODEL, IN_COLS), fixed, pipeline_mode=pl.Buffered(1)),
                  pl.BlockSpec((tm, LANES), row),
                  pl.BlockSpec((tm, LANES), row)],
        out_specs=[pl.BlockSpec((A_WIDTH, tm), lambda i: (0, i)),
                   pl.BlockSpec((tm, A_WIDTH), row),
                   pl.BlockSpec((A_HEADS, None, dv, tm), lambda i: (0, i // per_kv, 0, i % per_kv))]
                  + [pl.BlockSpec((tm, w), row) for w in widths[3:]],
        out_shape=[jax.ShapeDtypeStruct((A_WIDTH, seq), BF16),
                   jax.ShapeDtypeStruct((seq, A_WIDTH), BF16),
                   jax.ShapeDtypeStruct((A_HEADS, seq // tk, dv, tk), BF16)]
                  + [jax.ShapeDtypeStruct((seq, w), d) for w, d in zip(widths[3:], dtypes[3:])],
        compiler_params=pltpu.CompilerParams(dimension_semantics=("arbitrary",),
                                             vmem_limit_bytes=VMEM_LIMIT),
        name="inproj",
    )(x2, norm_g.reshape(1, D_MODEL), w_bf16, cos_t, sin_t)


LOG2E = math.log2(math.e)
A_ONES_ROWS = 16
A_QCHUNK = 256


def _attn_a_kernel(qt_ref, k_ref, vt_ref, za_ref, g_ref, lq1_ref, lk1_ref, lq2_ref, lk2_ref,
                   o_ref, qz_scr, m_scr, acc_scr, s_a, s_b, p_a, p_b, al_a, al_b,
                   *, tq, tk, lam_init):
    assert tq == tk
    i = pl.program_id(1)
    d = A_HEAD_DIM
    qt = qt_ref[...]
    row = lax.broadcasted_iota(jnp.int32, qt.shape, 0)
    zero = jnp.zeros_like(qt)
    q_maps = (jnp.where(row < d, qt, zero), jnp.where(row >= d, qt, zero))
    n_half = tq // A_QCHUNK
    for h in range(n_half):
        for m in range(2):
            qz_scr[m, h] = q_maps[m][:, h * A_QCHUNK:(h + 1) * A_QCHUNK]
    m_scr[...] = jnp.full(m_scr.shape, -jnp.inf, F32)
    acc_scr[...] = jnp.zeros(acc_scr.shape, F32)

    chunks = [(m, h) for h in range(n_half) for m in range(2)]

    def scores(j, s_buf, c):
        kb = k_ref[pl.ds(pl.multiple_of(j * tk, tk), tk), :]
        s_buf[c] = jnp.dot(kb, qz_scr[c], preferred_element_type=F32)

    def softmax(s_buf, p_buf, al_buf, masked, c):
        s = s_buf[c]
        if masked:
            keyi = lax.broadcasted_iota(jnp.int32, s.shape, 0)
            qryi = lax.broadcasted_iota(jnp.int32, s.shape, 1) + c[1] * A_QCHUNK
            s = jnp.where(keyi <= qryi, s, -jnp.inf)
        m_old = m_scr[c]
        m_new = jnp.maximum(m_old, jnp.max(s, axis=0, keepdims=True))
        p_buf[c] = jnp.exp2(s - m_new).astype(BF16)
        al_buf[c] = jnp.exp2(m_old - m_new)
        m_scr[c] = m_new

    def values(j, p_buf, al_buf, c):
        vtb = vt_ref[j]
        acc_scr[c] = al_buf[c] * acc_scr[c] + jnp.dot(vtb, p_buf[c], preferred_element_type=F32)

    def stage(val=None, sco=None, sm=None):
        for c in chunks:
            if sco is not None:
                scores(*sco, c)
            if val is not None:
                values(*val, c)
            if sm is not None:
                softmax(*sm, c)

    A = (p_a, al_a)
    B = (p_b, al_b)

    stage(sco=(0, s_a))

    @pl.when(i >= 1)
    def _():
        stage(sco=(1, s_b), sm=(s_a, *A, False))

    def pair(t):
        stage(sco=(t, s_a), val=(t - 2, *A), sm=(s_b, *B, False))
        stage(sco=(t + 1, s_b), val=(t - 1, *B), sm=(s_a, *A, False))

    def four_pairs(u, carry):
        for k in range(4):
            pair(2 + 8 * u + 2 * k)
        return carry

    n_pairs = jnp.maximum(i - 1, 0) // 2
    n_quads = n_pairs // 4
    lax.fori_loop(0, n_quads, four_pairs, 0)

    def one_pair(u, carry):
        pair(2 + 8 * n_quads + 2 * u)
        return carry

    lax.fori_loop(0, n_pairs % 4, one_pair, 0)

    @pl.when(i % 2 == 0)
    def _():
        @pl.when(i >= 2)
        def _():
            stage(sco=(i, s_a), val=(i - 2, *A), sm=(s_b, *B, False))
            stage(val=(i - 1, *B))
        stage(sm=(s_a, *A, True))
        stage(val=(i, *A))

    @pl.when(i % 2 == 1)
    def _():
        stage(val=(i - 1, *A))
        stage(sm=(s_b, *B, True))
        stage(val=(i, *B))

    lam = (jnp.exp(jnp.sum(lq1_ref[...] * lk1_ref[...], axis=-1, keepdims=True))
           - jnp.exp(jnp.sum(lq2_ref[...] * lk2_ref[...], axis=-1, keepdims=True)) + lam_init)
    dv = 2 * A_HEAD_DIM
    acc = [jnp.concatenate([acc_scr[m, h] for h in range(n_half)], axis=1) for m in range(2)]
    o0 = acc[0][0:dv] * (1.0 / acc[0][dv:dv + 1])
    o1 = acc[1][0:dv] * (1.0 / acc[1][dv:dv + 1])
    o = o0 - lam * o1
    y = o * lax.rsqrt(jnp.mean(o * o, axis=0, keepdims=True) + EPS) * g_ref[...]
    y = (y * (1.0 - lam_init)).T
    o_ref[...] = (y * za_ref[...].astype(F32)).astype(o_ref.dtype)


def _attn_a(qt, ka, vt, za, subln_g, lq1, lk1, lq2, lk2, lam_init, tq=ATTN_A_BLOCK):
    seq = ka.shape[0]
    nk, tk = vt.shape[1], vt.shape[3]
    chunked = lambda rows, dtype: pltpu.VMEM((2, tq // A_QCHUNK, rows, A_QCHUNK), dtype)
    fixed = lambda h, i: (0, 0)
    vec = lambda a: a.reshape(1, -1)
    return pl.pallas_call(
        functools.partial(_attn_a_kernel, tq=tq, tk=tk, lam_init=lam_init),
        grid=(A_HEADS, seq // tq),
        in_specs=[pl.BlockSpec((LANES, tq), lambda h, i: (h, i)),
                  pl.BlockSpec((seq, LANES), lambda h, i: (0, h)),
                  pl.BlockSpec((None, nk, LANES + A_ONES_ROWS, tk), lambda h, i: (h, 0, 0, 0)),
                  pl.BlockSpec((tq, LANES), lambda h, i: (i, h)),
                  pl.BlockSpec((2 * A_HEAD_DIM, 1), fixed),
                  pl.BlockSpec((1, A_HEAD_DIM), fixed),
                  pl.BlockSpec((1, A_HEAD_DIM), fixed),
                  pl.BlockSpec((1, A_HEAD_DIM), fixed),
                  pl.BlockSpec((1, A_HEAD_DIM), fixed)],
        out_specs=pl.BlockSpec((tq, LANES), lambda h, i: (i, h)),
        out_shape=jax.ShapeDtypeStruct((seq, A_WIDTH), BF16),
        scratch_shapes=[chunked(LANES, BF16),
                        chunked(1, F32),
                        chunked(2 * A_HEAD_DIM + A_ONES_ROWS, F32),
                        chunked(tk, F32), chunked(tk, F32),
                        chunked(tk, BF16), chunked(tk, BF16),
                        chunked(1, F32), chunked(1, F32)],
        compiler_params=pltpu.CompilerParams(dimension_semantics=("arbitrary", "arbitrary"),
                                             vmem_limit_bytes=VMEM_LIMIT),
        name="attn_a",
    )(qt, ka, vt, za, subln_g.reshape(-1, 1), vec(lq1), vec(lk1), vec(lq2), vec(lk2))


C_TILE = C_BLOCK * max(dil for _, dil in C_PATTERNS)


def _attn_c_kernel(q_ref, kp_ref, kc_ref, vp_ref, vc_ref, z_ref, bias_ref, o_ref, out_scr, lse_scr):
    first_tile = (pl.program_id(1) == 0).astype(jnp.int32)
    head0 = lax.broadcasted_iota(jnp.int32, (LANES, C_BLOCK), 0) < C_HEAD_DIM

    def pick(t):
        return jnp.where(head0, t[:, :C_BLOCK], t[:, C_BLOCK:])

    def strided(start, dil):
        return pl.ds(start, C_BLOCK) if dil == 1 else pl.ds(start, C_BLOCK, stride=dil)

    for p, (window, dil) in enumerate(C_PATTERNS):
        assert window // dil == C_BLOCK
        span = C_BLOCK * dil
        n_blocks = C_TILE // C_BLOCK
        idx = []
        for b in range(n_blocks):
            u, r = divmod(b, dil)
            older = strided(C_TILE - span + r, dil) if u == 0 else strided((u - 1) * span + r, dil)
            idx.append((strided(u * span + r, dil), older, u == 0))

        def keys(prev_ref, cur_ref, q_rows, older, in_prev):
            return jnp.concatenate([(prev_ref if in_prev else cur_ref)[older, :], cur_ref[q_rows, :]], axis=0)

        q_both, v_t, s, e, stats, o_all = [], [], [], [], [], []
        for q_rows, older, in_prev in idx:
            qt = q_ref[q_rows, :].T
            zero = jnp.zeros_like(qt)
            q_both.append(jnp.concatenate([jnp.where(head0, qt, zero), jnp.where(head0, zero, qt)],
                                          axis=1).astype(BF16))
            v_t.append(keys(vp_ref, vc_ref, q_rows, older, in_prev).T.astype(BF16))
        for k, (q_rows, older, in_prev) in enumerate(idx):
            k2 = keys(kp_ref, kc_ref, q_rows, older, in_prev).astype(BF16)
            bias = bias_ref[first_tile] if in_prev else bias_ref[0]
            s.append(jnp.dot(k2, q_both[k], preferred_element_type=F32) + bias)
        for k in range(n_blocks):
            mx = jnp.max(s[k], axis=0, keepdims=True)
            ek = jnp.exp2(s[k] - mx)
            den = jnp.sum(ek, axis=0, keepdims=True)
            e.append(ek.astype(BF16))
            stats.append((1.0 / den, mx + jnp.log2(den)))
        for k in range(n_blocks):
            o_all.append(jnp.dot(v_t[k], e[k], preferred_element_type=F32))
        wide = (LANES, 2 * C_BLOCK)
        for k, (q_rows, _, _) in enumerate(idx):
            inv, lse = stats[k]
            o_t = pick(o_all[k]) * pick(jnp.broadcast_to(inv, wide))
            l_t = pick(jnp.broadcast_to(lse, wide))
            out_scr[p, q_rows, :] = o_t.T
            lse_scr[p, q_rows, :] = l_t.T

    l0, l1, l2 = lse_scr[0], lse_scr[1], lse_scr[2]
    mx = jnp.maximum(jnp.maximum(l0, l1), l2)
    w0 = jnp.exp2(l0 - mx)
    w1 = jnp.exp2(l1 - mx)
    w2 = jnp.exp2(l2 - mx)
    oc = (w0 * out_scr[0] + w1 * out_scr[1] + w2 * out_scr[2]) / (w0 + w1 + w2)
    o_ref[...] = (oc * z_ref[...].astype(F32)).astype(o_ref.dtype)


def _attn_c_bias():
    ki = np.arange(2 * C_BLOCK)[:, None]
    qi = np.tile(np.arange(C_BLOCK), LANES // C_HEAD_DIM)[None, :]
    rel = qi + C_BLOCK - ki
    band = (rel >= 0) & (rel <= C_BLOCK)
    masks = np.stack([band, band & (ki >= C_BLOCK)])
    return jnp.asarray(np.where(masks, 0.0, -np.inf), dtype=F32)


def _attn_c(qc, kc, vc, zc):
    seq = qc.shape[0]
    cur = lambda c, t: (t, c)
    prev = lambda c, t: (jnp.maximum(t - 1, 0), c)
    tile = (C_TILE, LANES)
    return pl.pallas_call(
        _attn_c_kernel,
        grid=(C_WIDTH // LANES, seq // C_TILE),
        in_specs=[pl.BlockSpec(tile, cur),
                  pl.BlockSpec(tile, prev), pl.BlockSpec(tile, cur),
                  pl.BlockSpec(tile, prev), pl.BlockSpec(tile, cur),
                  pl.BlockSpec(tile, cur),
                  pl.BlockSpec((2, 2 * C_BLOCK, 2 * C_BLOCK), lambda c, t: (0, 0, 0))],
        out_specs=pl.BlockSpec(tile, cur),
        out_shape=jax.ShapeDtypeStruct((seq, C_WIDTH), BF16),
        scratch_shapes=[pltpu.VMEM((len(C_PATTERNS), C_TILE, LANES), F32),
                        pltpu.VMEM((len(C_PATTERNS), C_TILE, LANES), F32)],
        compiler_params=pltpu.CompilerParams(dimension_semantics=("arbitrary", "arbitrary"),
                                             vmem_limit_bytes=VMEM_LIMIT),
        name="attn_c",
    )(qc, kc, kc, vc, vc, zc, _attn_c_bias())


def _merge_kernel(x_ref, ya_ref, yc_ref, guv_ref, zb_ref, gate_ref, lng_ref, lnb_ref,
                  sw_ref, sb_ref, wb_ref, wo_ref, fg_ref, o_ref, *, final_norm):
    tm = x_ref.shape[0]
    guv = guv_ref[...]
    u = guv[:, :B_WIDTH].astype(F32)
    vb = guv[:, B_WIDTH:].astype(F32)
    mu = jnp.mean(vb, axis=-1, keepdims=True)
    var = jnp.mean(jnp.square(vb - mu), axis=-1, keepdims=True)
    vb = ((vb - mu) * lax.rsqrt(var + EPS) * lng_ref[...] + lnb_ref[...]).astype(BF16)
    ti = lax.broadcasted_iota(jnp.int32, (B_CHUNK, B_CHUNK), 0)
    si = lax.broadcasted_iota(jnp.int32, (B_CHUNK, B_CHUNK), 1)
    causal = si <= ti
    rows = []
    for c in range(tm // B_CHUNK):
        cols = []
        for g in range(B_GROUPS):
            w = jnp.where(causal, sw_ref[g], 0.0).astype(BF16)
            blk = vb[c * B_CHUNK:(c + 1) * B_CHUNK, g * B_GROUP_DIM:(g + 1) * B_GROUP_DIM]
            mixed = jnp.dot(w, blk, preferred_element_type=F32) + sb_ref[:, g:g + 1]
            cols.append(mixed)
        rows.append(jnp.concatenate(cols, axis=-1))
    mixed = jnp.concatenate(rows, axis=0)
    yb = (u * mixed * zb_ref[...].astype(F32)).astype(BF16)

    gates = gate_ref[...]
    merged = jnp.zeros((tm, D_MODEL), F32)
    for n, y in enumerate((ya_ref[...], yb, yc_ref[...])):
        pb = jnp.dot(y, wb_ref[n], preferred_element_type=F32)
        merged = merged + gates[:, n * D_MODEL:(n + 1) * D_MODEL].astype(F32) * pb
    out = x_ref[...] + jnp.dot(merged.astype(BF16), wo_ref[...], preferred_element_type=F32)
    if final_norm:
        out = out * lax.rsqrt(jnp.mean(out * out, axis=-1, keepdims=True) + EPS) * fg_ref[...]
    o_ref[...] = out


def _merge(x2, ya, yc, guv, zb, gates, ln_g, ln_b, sgu_w, sgu_b, wb_bf16, wo_bf16, final_g,
           final_norm, tm=MERGE_ROWS):
    seq = x2.shape[0]
    row = lambda i: (i, 0)
    fixed2 = lambda i: (0, 0)
    fixed3 = lambda i: (0, 0, 0)
    return pl.pallas_call(
        functools.partial(_merge_kernel, final_norm=final_norm),
        grid=(seq // tm,),
        in_specs=[pl.BlockSpec((tm, D_MODEL), row),
                  pl.BlockSpec((tm, A_WIDTH), row),
                  pl.BlockSpec((tm, C_WIDTH), row),
                  pl.BlockSpec((tm, 2 * B_WIDTH), row),
                  pl.BlockSpec((tm, B_WIDTH), row),
                  pl.BlockSpec((tm, N_BRANCH * D_MODEL), row),
                  pl.BlockSpec((1, B_WIDTH), fixed2),
                  pl.BlockSpec((1, B_WIDTH), fixed2),
                  pl.BlockSpec((B_GROUPS, B_CHUNK, B_CHUNK), fixed3),
                  pl.BlockSpec((B_CHUNK, B_GROUPS), fixed2),
                  pl.BlockSpec((N_BRANCH, BRANCH_WIDTH, D_MODEL), fixed3),
                  pl.BlockSpec((D_MODEL, D_MODEL), fixed2),
                  pl.BlockSpec((1, D_MODEL), fixed2)],
        out_specs=pl.BlockSpec((tm, D_MODEL), row),
        out_shape=jax.ShapeDtypeStruct((seq, D_MODEL), F32),
        compiler_params=pltpu.CompilerParams(dimension_semantics=("arbitrary",),
                                             vmem_limit_bytes=VMEM_LIMIT),
        name="merge",
    )(x2, ya, yc, guv, zb, gates, ln_g.reshape(1, -1), ln_b.reshape(1, -1), sgu_w, sgu_b.T,
      wb_bf16, wo_bf16, final_g.reshape(1, -1))


def kernel(x, positions, norm_g, w_in, lam_q1, lam_k1, lam_q2, lam_k2, subln_g, sgu_ln_g, sgu_ln_b,
           sgu_w, sgu_b, w_branch, w_out, final_g):
    batch, seq, _ = x.shape
    depth = norm_g.shape[0]
    assert batch == 1 and seq % C_TILE == 0
    h = x.reshape(seq, D_MODEL)
    cos_t, sin_t = _rope_tables(positions.reshape(seq), seq)
    for l in range(depth):
        lam_init = 0.8 - 0.6 * math.exp(-0.3 * l)
        (qt, ka, vt, za, guv, zb, qc, kc, vc, zc, gates) = _inproj(
            h, norm_g[l], w_in[l].astype(BF16), cos_t, sin_t)
        ya = _attn_a(qt, ka, vt, za, subln_g[l], lam_q1[l], lam_k1[l], lam_q2[l], lam_k2[l], lam_init)
        yc = _attn_c(qc, kc, vc, zc)
        h = _merge(h, ya, yc, guv, zb, gates, sgu_ln_g[l], sgu_ln_b[l], sgu_w[l], sgu_b[l],
                   w_branch[l].astype(BF16), w_out[l].astype(BF16), final_g,
                   final_norm=(l == depth - 1))
    return h.reshape(batch, seq, D_MODEL)
```

```python
import functools
import math

import jax
import jax.numpy as jnp
import numpy as np
from jax import lax
from jax.experimental import pallas as pl
from jax.experimental.pallas import tpu as pltpu

D_MODEL = 1024
A_HEADS = 4
A_HEAD_DIM = 64
A_WIDTH = A_HEADS * 2 * A_HEAD_DIM
B_WIDTH = 512
B_GROUPS = 4
B_GROUP_DIM = B_WIDTH // B_GROUPS
B_CHUNK = 128
C_HEADS = 8
C_HEAD_DIM = 64
C_WIDTH = C_HEADS * C_HEAD_DIM
C_PATTERNS = ((128, 1), (512, 4), (2048, 16))
C_BLOCK = 128
N_BRANCH = 3
BRANCH_WIDTH = 512
ROPE_THETA = 500000.0
ROPE_FRAC = 4
EPS = 1e-6

LANES = 128
VMEM_LIMIT = 56 * 1024 * 1024

INPROJ_ROWS = 256
ATTN_A_BLOCK = 512
MERGE_ROWS = 512

_SIZES = (A_WIDTH, A_WIDTH, A_WIDTH, A_WIDTH, 2 * B_WIDTH, B_WIDTH,
          C_WIDTH, C_WIDTH, C_WIDTH, C_WIDTH, N_BRANCH * D_MODEL)
_OFFS = tuple(int(sum(_SIZES[:i])) for i in range(len(_SIZES)))
IN_COLS = int(sum(_SIZES))

F32 = jnp.float32
BF16 = jnp.bfloat16


def _silu(t):
    return t * (1.0 / (1.0 + jnp.exp(-t)))


def _sigmoid(t):
    return 1.0 / (1.0 + jnp.exp(-t))


def _gelu_tanh(t):
    c = math.sqrt(2.0 / math.pi)
    return 0.5 * t * (1.0 + jnp.tanh(c * (t + 0.044715 * (t * t * t))))


def _rope_table_kernel(pos_ref, inv_ref, sgn_ref, cos_ref, sin_ref):
    ang = pos_ref[...].astype(F32) * inv_ref[...]
    cos_ref[...] = jnp.cos(ang)
    sin_ref[...] = jnp.sin(ang) * sgn_ref[...]


def _rope_tables(positions, seq):
    rot = A_HEAD_DIM // ROPE_FRAC
    half = rot // 2
    inv = jnp.power(jnp.float32(ROPE_THETA), -jnp.arange(half, dtype=jnp.float32) * 2.0 / rot)
    zeros = jnp.zeros((A_HEAD_DIM - rot,), F32)
    inv_head = jnp.concatenate([inv, inv, zeros])
    sgn_head = jnp.concatenate([-jnp.ones((half,), F32), jnp.ones((half,), F32), zeros])
    reps = LANES // A_HEAD_DIM
    inv_lane = jnp.tile(inv_head, reps).reshape(1, LANES)
    sgn_lane = jnp.tile(sgn_head, reps).reshape(1, LANES)
    tm = 1024
    return pl.pallas_call(
        _rope_table_kernel,
        grid=(seq // tm,),
        in_specs=[pl.BlockSpec((tm, 1), lambda i: (i, 0)),
                  pl.BlockSpec((1, LANES), lambda i: (0, 0)),
                  pl.BlockSpec((1, LANES), lambda i: (0, 0))],
        out_specs=[pl.BlockSpec((tm, LANES), lambda i: (i, 0)),
                   pl.BlockSpec((tm, LANES), lambda i: (i, 0))],
        out_shape=[jax.ShapeDtypeStruct((seq, LANES), F32)] * 2,
        name="rope_tables",
    )(positions.reshape(seq, 1), inv_lane, sgn_lane)


def _inproj_kernel(x_ref, g_ref, w_ref, cos_ref, sin_ref,
                   qt_ref, ka_ref, vt_ref, za_ref, guv_ref, zb_ref,
                   qc_ref, kc_ref, vc_ref, zc_ref, gate_ref):
    x = x_ref[...]
    h = x * lax.rsqrt(jnp.mean(x * x, axis=-1, keepdims=True) + EPS) * g_ref[...]
    h = h.astype(BF16)
    cos = cos_ref[...]
    sin = sin_ref[...]
    tm = x.shape[0]
    lane = lax.broadcasted_iota(jnp.int32, (tm, LANES), 1)
    first_half = (lane % A_HEAD_DIM) < (A_HEAD_DIM // ROPE_FRAC // 2)
    shift = A_HEAD_DIM // ROPE_FRAC // 2

    def proj(lo, n):
        return jnp.dot(h, w_ref[:, lo:lo + n], preferred_element_type=F32)

    def rope_store(seg, out_ref, scale, width=A_WIDTH, transposed=False):
        full = proj(_OFFS[seg], width)
        for c in range(width // LANES):
            t = full[:, c * LANES:(c + 1) * LANES]
            partner = jnp.where(first_half, pltpu.roll(t, LANES - shift, 1), pltpu.roll(t, shift, 1))
            r = t * cos + partner * sin
            if scale != 1.0:
                r = r * scale
            if transposed:
                out_ref[c * LANES:(c + 1) * LANES, :] = r.T.astype(out_ref.dtype)
            else:
                out_ref[:, c * LANES:(c + 1) * LANES] = r.astype(out_ref.dtype)

    def act_store(seg, out_ref, fn, width=512):
        n = out_ref.shape[1]
        for c in range(n // width):
            t = proj(_OFFS[seg] + c * width, width)
            out_ref[:, c * width:(c + 1) * width] = fn(t).astype(out_ref.dtype)

    ident = lambda t: t
    rope_store(0, qt_ref, LOG2E / math.sqrt(A_HEAD_DIM), transposed=True)
    rope_store(1, ka_ref, 1.0)
    va = proj(_OFFS[2], A_WIDTH)
    dv = 2 * A_HEAD_DIM
    for hd in range(A_HEADS):
        vt_ref[hd, 0:dv, :] = va[:, hd * dv:(hd + 1) * dv].T.astype(vt_ref.dtype)
        vt_ref[hd, dv:dv + A_ONES_ROWS, :] = jnp.ones((A_ONES_ROWS, tm), vt_ref.dtype)
    act_store(3, za_ref, _silu)
    act_store(4, guv_ref, _gelu_tanh)
    act_store(5, zb_ref, _silu)
    rope_store(6, qc_ref, LOG2E / math.sqrt(C_HEAD_DIM))
    rope_store(7, kc_ref, 1.0)
    act_store(8, vc_ref, ident)
    act_store(9, zc_ref, _silu)
    act_store(10, gate_ref, _sigmoid)


def _inproj(x2, norm_g, w_bf16, cos_t, sin_t, tm=INPROJ_ROWS, tk=ATTN_A_BLOCK):
    seq = x2.shape[0]
    row = lambda i: (i, 0)
    fixed = lambda i: (0, 0)
    per_kv = tk // tm
    dv = 2 * A_HEAD_DIM + A_ONES_ROWS
    widths = (A_WIDTH, A_WIDTH, A_WIDTH, A_WIDTH, 2 * B_WIDTH, B_WIDTH,
              C_WIDTH, C_WIDTH, C_WIDTH, C_WIDTH, N_BRANCH * D_MODEL)
    dtypes = (BF16, BF16, BF16, BF16, BF16, BF16, F32, F32, F32, BF16, BF16)
    return pl.pallas_call(
        _inproj_kernel,
        grid=(seq // tm,),
        in_specs=[pl.BlockSpec((tm, D_MODEL), row),
                  pl.BlockSpec((1, D_MODEL), fixed),
                  pl.BlockSpec((D_MODEL, IN_COLS), fixed, pipeline_mode=pl.Buffered(1)),
                  pl.BlockSpec((tm, LANES), row),
                  pl.BlockSpec((tm, LANES), row)],
        out_specs=[pl.BlockSpec((A_WIDTH, tm), lambda i: (0, i)),
                   pl.BlockSpec((tm, A_WIDTH), row),
                   pl.BlockSpec((A_HEADS, None, dv, tm), lambda i: (0, i // per_kv, 0, i % per_kv))]
                  + [pl.BlockSpec((tm, w), row) for w in widths[3:]],
        out_shape=[jax.ShapeDtypeStruct((A_WIDTH, seq), BF16),
                   jax.ShapeDtypeStruct((seq, A_WIDTH), BF16),
                   jax.ShapeDtypeStruct((A_HEADS, seq // tk, dv, tk), BF16)]
                  + [jax.ShapeDtypeStruct((seq, w), d) for w, d in zip(widths[3:], dtypes[3:])],
        compiler_params=pltpu.CompilerParams(dimension_semantics=("arbitrary",),
                                             vmem_limit_bytes=VMEM_LIMIT),
        name="inproj",
    )(x2, norm_g.reshape(1, D_MODEL), w_bf16, cos_t, sin_t)


LOG2E = math.log2(math.e)
A_ONES_ROWS = 16
A_QCHUNK = 256


def _attn_a_kernel(qt_ref, k_ref, vt_ref, za_ref, g_ref, lq1_ref, lk1_ref, lq2_ref, lk2_ref,
                   o_ref, qz_scr, m_scr, acc_scr, s_a, s_b, p_a, p_b, al_a, al_b,
                   *, tq, tk, lam_init):
    i = pl.program_id(1)
    d = A_HEAD_DIM
    qt = qt_ref[...]
    row = lax.broadcasted_iota(jnp.int32, qt.shape, 0)
    zero = jnp.zeros_like(qt)
    q_maps = (jnp.where(row < d, qt, zero), jnp.where(row >= d, qt, zero))
    n_half = tq // A_QCHUNK
    for h in range(n_half):
        for m in range(2):
            qz_scr[m, h] = q_maps[m][:, h * A_QCHUNK:(h + 1) * A_QCHUNK]
    m_scr[...] = jnp.full(m_scr.shape, -jnp.inf, F32)
    acc_scr[...] = jnp.zeros(acc_scr.shape, F32)

    chunks = [(m, h) for h in range(n_half) for m in range(2)]

    def scores(j, s_buf, c):
        kb = k_ref[pl.ds(pl.multiple_of(j * tk, tk), tk), :]
        s_buf[c] = jnp.dot(kb, qz_scr[c], preferred_element_type=F32)

    def softmax(s_buf, p_buf, al_buf, masked, c):
        s = s_buf[c]
        if masked is not None:
            keyi = lax.broadcasted_iota(jnp.int32, s.shape, 0) + masked
            qryi = lax.broadcasted_iota(jnp.int32, s.shape, 1) + c[1] * A_QCHUNK
            s = jnp.where(keyi <= qryi, s, -jnp.inf)
        m_old = m_scr[c]
        m_new = jnp.maximum(m_old, jnp.max(s, axis=0, keepdims=True))
        p_buf[c] = jnp.exp2(s - m_new).astype(BF16)
        al_buf[c] = jnp.exp2(m_old - m_new)
        m_scr[c] = m_new

    def values(j, p_buf, al_buf, c):
        vtb = vt_ref[j]
        acc_scr[c] = al_buf[c] * acc_scr[c] + jnp.dot(vtb, p_buf[c], preferred_element_type=F32)

    def stage(val=None, sco=None, sm=None):
        for c in chunks:
            if sco is not None:
                scores(*sco, c)
            if val is not None:
                values(*val, c)
            if sm is not None:
                softmax(*sm, c)

    A = (p_a, al_a)
    B = (p_b, al_b)

    assert tq == 2 * tk
    last = 2 * i + 1
    stage(sco=(0, s_a))

    @pl.when(i == 0)
    def _():
        stage(sco=(1, s_b), sm=(s_a, *A, 0))

    @pl.when(i >= 1)
    def _():
        stage(sco=(1, s_b), sm=(s_a, *A, None))

    def pair(t):
        stage(sco=(t, s_a), val=(t - 2, *A), sm=(s_b, *B, None))
        stage(sco=(t + 1, s_b), val=(t - 1, *B), sm=(s_a, *A, None))

    def four_pairs(u, carry):
        for k in range(4):
            pair(2 + 8 * u + 2 * k)
        return carry

    n_pairs = jnp.maximum(i - 1, 0)
    n_quads = n_pairs // 4
    lax.fori_loop(0, n_quads, four_pairs, 0)

    def one_pair(u, carry):
        pair(2 + 8 * n_quads + 2 * u)
        return carry

    lax.fori_loop(0, n_pairs % 4, one_pair, 0)

    @pl.when(i >= 1)
    def _():
        stage(sco=(last - 1, s_a), val=(last - 3, *A), sm=(s_b, *B, None))
        stage(sco=(last, s_b), val=(last - 2, *B), sm=(s_a, *A, 0))

    stage(val=(last - 1, *A))
    stage(sm=(s_b, *B, tk))
    stage(val=(last, *B))

    lam = (jnp.exp(jnp.sum(lq1_ref[...] * lk1_ref[...], axis=-1, keepdims=True))
           - jnp.exp(jnp.sum(lq2_ref[...] * lk2_ref[...], axis=-1, keepdims=True)) + lam_init)
    dv = 2 * A_HEAD_DIM
    acc = [jnp.concatenate([acc_scr[m, h] for h in range(n_half)], axis=1) for m in range(2)]
    o0 = acc[0][0:dv] * (1.0 / acc[0][dv:dv + 1])
    o1 = acc[1][0:dv] * (1.0 / acc[1][dv:dv + 1])
    o = o0 - lam * o1
    y = o * lax.rsqrt(jnp.mean(o * o, axis=0, keepdims=True) + EPS) * g_ref[...]
    y = (y * (1.0 - lam_init)).T
    o_ref[...] = (y * za_ref[...].astype(F32)).astype(o_ref.dtype)


def _attn_a(qt, ka, vt, za, subln_g, lq1, lk1, lq2, lk2, lam_init, tq=2 * ATTN_A_BLOCK):
    seq = ka.shape[0]
    nk, tk = vt.shape[1], vt.shape[3]
    chunked = lambda rows, dtype: pltpu.VMEM((2, tq // A_QCHUNK, rows, A_QCHUNK), dtype)
    fixed = lambda h, i: (0, 0)
    vec = lambda a: a.reshape(1, -1)
    return pl.pallas_call(
        functools.partial(_attn_a_kernel, tq=tq, tk=tk, lam_init=lam_init),
        grid=(A_HEADS, seq // tq),
        in_specs=[pl.BlockSpec((LANES, tq), lambda h, i: (h, i)),
                  pl.BlockSpec((seq, LANES), lambda h, i: (0, h)),
                  pl.BlockSpec((None, nk, LANES + A_ONES_ROWS, tk), lambda h, i: (h, 0, 0, 0)),
                  pl.BlockSpec((tq, LANES), lambda h, i: (i, h)),
                  pl.BlockSpec((2 * A_HEAD_DIM, 1), fixed),
                  pl.BlockSpec((1, A_HEAD_DIM), fixed),
                  pl.BlockSpec((1, A_HEAD_DIM), fixed),
                  pl.BlockSpec((1, A_HEAD_DIM), fixed),
                  pl.BlockSpec((1, A_HEAD_DIM), fixed)],
        out_specs=pl.BlockSpec((tq, LANES), lambda h, i: (i, h)),
        out_shape=jax.ShapeDtypeStruct((seq, A_WIDTH), BF16),
        scratch_shapes=[chunked(LANES, BF16),
                        chunked(1, F32),
                        chunked(2 * A_HEAD_DIM + A_ONES_ROWS, F32),
                        chunked(tk, F32), chunked(tk, F32),
                        chunked(tk, BF16), chunked(tk, BF16),
                        chunked(1, F32), chunked(1, F32)],
        compiler_params=pltpu.CompilerParams(dimension_semantics=("arbitrary", "arbitrary"),
                                             vmem_limit_bytes=VMEM_LIMIT),
        name="attn_a",
    )(qt, ka, vt, za, subln_g.reshape(-1, 1), vec(lq1), vec(lk1), vec(lq2), vec(lk2))


C_TILE = C_BLOCK * max(dil for _, dil in C_PATTERNS)


def _attn_c_kernel(q_ref, kp_ref, kc_ref, vp_ref, vc_ref, z_ref, bias_ref, o_ref, out_scr, lse_scr):
    first_tile = (pl.program_id(1) == 0).astype(jnp.int32)
    head0 = lax.broadcasted_iota(jnp.int32, (LANES, C_BLOCK), 0) < C_HEAD_DIM

    def pick(t):
        return jnp.where(head0, t[:, :C_BLOCK], t[:, C_BLOCK:])

    def strided(start, dil):
        return pl.ds(start, C_BLOCK) if dil == 1 else pl.ds(start, C_BLOCK, stride=dil)

    for p, (window, dil) in enumerate(C_PATTERNS):
        assert window // dil == C_BLOCK
        span = C_BLOCK * dil
        n_blocks = C_TILE // C_BLOCK
        idx = []
        for b in range(n_blocks):
            u, r = divmod(b, dil)
            older = strided(C_TILE - span + r, dil) if u == 0 else strided((u - 1) * span + r, dil)
            idx.append((strided(u * span + r, dil), older, u == 0))

        def keys(prev_ref, cur_ref, q_rows, older, in_prev):
            return jnp.concatenate([(prev_ref if in_prev else cur_ref)[older, :], cur_ref[q_rows, :]], axis=0)

        q_both, v_t, s, e, stats, o_all = [], [], [], [], [], []
        for q_rows, older, in_prev in idx:
            qt = q_ref[q_rows, :].T
            zero = jnp.zeros_like(qt)
            q_both.append(jnp.concatenate([jnp.where(head0, qt, zero), jnp.where(head0, zero, qt)],
                                          axis=1).astype(BF16))
            v_t.append(keys(vp_ref, vc_ref, q_rows, older, in_prev).T.astype(BF16))
        for k, (q_rows, older, in_prev) in enumerate(idx):
            k2 = keys(kp_ref, kc_ref, q_rows, older, in_prev).astype(BF16)
            bias = bias_ref[first_tile] if in_prev else bias_ref[0]
            s.append(jnp.dot(k2, q_both[k], preferred_element_type=F32) + bias)
        for k in range(n_blocks):
            mx = jnp.max(s[k], axis=0, keepdims=True)
            ek = jnp.exp2(s[k] - mx)
            den = jnp.sum(ek, axis=0, keepdims=True)
            e.append(ek.astype(BF16))
            stats.append((1.0 / den, mx + jnp.log2(den)))
        for k in range(n_blocks):
            o_all.append(jnp.dot(v_t[k], e[k], preferred_element_type=F32))
        wide = (LANES, 2 * C_BLOCK)
        for k, (q_rows, _, _) in enumerate(idx):
            inv, lse = stats[k]
            o_t = pick(o_all[k]) * pick(jnp.broadcast_to(inv, wide))
            l_t = pick(jnp.broadcast_to(lse, wide))
            out_scr[p, q_rows, :] = o_t.T
            lse_scr[p, q_rows, :] = l_t.T

    l0, l1, l2 = lse_scr[0], lse_scr[1], lse_scr[2]
    mx = jnp.maximum(jnp.maximum(l0, l1), l2)
    w0 = jnp.exp2(l0 - mx)
    w1 = jnp.exp2(l1 - mx)
    w2 = jnp.exp2(l2 - mx)
    oc = (w0 * out_scr[0] + w1 * out_scr[1] + w2 * out_scr[2]) / (w0 + w1 + w2)
    o_ref[...] = (oc * z_ref[...].astype(F32)).astype(o_ref.dtype)


def _attn_c_bias():
    ki = np.arange(2 * C_BLOCK)[:, None]
    qi = np.tile(np.arange(C_BLOCK), LANES // C_HEAD_DIM)[None, :]
    rel = qi + C_BLOCK - ki
    band = (rel >= 0) & (rel <= C_BLOCK)
    masks = np.stack([band, band & (ki >= C_BLOCK)])
    return jnp.asarray(np.where(masks, 0.0, -np.inf), dtype=F32)


def _attn_c(qc, kc, vc, zc):
    seq = qc.shape[0]
    cur = lambda c, t: (t, c)
    prev = lambda c, t: (jnp.maximum(t - 1, 0), c)
    tile = (C_TILE, LANES)
    return pl.pallas_call(
        _attn_c_kernel,
        grid=(C_WIDTH // LANES, seq // C_TILE),
        in_specs=[pl.BlockSpec(tile, cur),
                  pl.BlockSpec(tile, prev), pl.BlockSpec(tile, cur),
                  pl.BlockSpec(tile, prev), pl.BlockSpec(tile, cur),
                  pl.BlockSpec(tile, cur),
                  pl.BlockSpec((2, 2 * C_BLOCK, 2 * C_BLOCK), lambda c, t: (0, 0, 0))],
        out_specs=pl.BlockSpec(tile, cur),
        out_shape=jax.ShapeDtypeStruct((seq, C_WIDTH), BF16),
        scratch_shapes=[pltpu.VMEM((len(C_PATTERNS), C_TILE, LANES), F32),
                        pltpu.VMEM((len(C_PATTERNS), C_TILE, LANES), F32)],
        compiler_params=pltpu.CompilerParams(dimension_semantics=("arbitrary", "arbitrary"),
                                             vmem_limit_bytes=VMEM_LIMIT),
        name="attn_c",
    )(qc, kc, kc, vc, vc, zc, _attn_c_bias())


def _merge_kernel(x_ref, ya_ref, yc_ref, guv_ref, zb_ref, gate_ref, lng_ref, lnb_ref,
                  sw_ref, sb_ref, wb_ref, wo_ref, fg_ref, o_ref, *, final_norm):
    tm = x_ref.shape[0]
    guv = guv_ref[...]
    u = guv[:, :B_WIDTH].astype(F32)
    vb = guv[:, B_WIDTH:].astype(F32)
    mu = jnp.mean(vb, axis=-1, keepdims=True)
    var = jnp.mean(jnp.square(vb - mu), axis=-1, keepdims=True)
    vb = ((vb - mu) * lax.rsqrt(var + EPS) * lng_ref[...] + lnb_ref[...]).astype(BF16)
    ti = lax.broadcasted_iota(jnp.int32, (B_CHUNK, B_CHUNK), 0)
    si = lax.broadcasted_iota(jnp.int32, (B_CHUNK, B_CHUNK), 1)
    causal = si <= ti
    rows = []
    for c in range(tm // B_CHUNK):
        cols = []
        for g in range(B_GROUPS):
            w = jnp.where(causal, sw_ref[g], 0.0).astype(BF16)
            blk = vb[c * B_CHUNK:(c + 1) * B_CHUNK, g * B_GROUP_DIM:(g + 1) * B_GROUP_DIM]
            mixed = jnp.dot(w, blk, preferred_element_type=F32) + sb_ref[:, g:g + 1]
            cols.append(mixed)
        rows.append(jnp.concatenate(cols, axis=-1))
    mixed = jnp.concatenate(rows, axis=0)
    yb = (u * mixed * zb_ref[...].astype(F32)).astype(BF16)

    gates = gate_ref[...]
    merged = jnp.zeros((tm, D_MODEL), F32)
    for n, y in enumerate((ya_ref[...], yb, yc_ref[...])):
        pb = jnp.dot(y, wb_ref[n], preferred_element_type=F32)
        merged = merged + gates[:, n * D_MODEL:(n + 1) * D_MODEL].astype(F32) * pb
    out = x_ref[...] + jnp.dot(merged.astype(BF16), wo_ref[...], preferred_element_type=F32)
    if final_norm:
        out = out * lax.rsqrt(jnp.mean(out * out, axis=-1, keepdims=True) + EPS) * fg_ref[...]
    o_ref[...] = out


def _merge(x2, ya, yc, guv, zb, gates, ln_g, ln_b, sgu_w, sgu_b, wb_bf16, wo_bf16, final_g,
           final_norm, tm=MERGE_ROWS):
    seq = x2.shape[0]
    row = lambda i: (i, 0)
    fixed2 = lambda i: (0, 0)
    fixed3 = lambda i: (0, 0, 0)
    return pl.pallas_call(
        functools.partial(_merge_kernel, final_norm=final_norm),
        grid=(seq // tm,),
        in_specs=[pl.BlockSpec((tm, D_MODEL), row),
                  pl.BlockSpec((tm, A_WIDTH), row),
                  pl.BlockSpec((tm, C_WIDTH), row),
                  pl.BlockSpec((tm, 2 * B_WIDTH), row),
                  pl.BlockSpec((tm, B_WIDTH), row),
                  pl.BlockSpec((tm, N_BRANCH * D_MODEL), row),
                  pl.BlockSpec((1, B_WIDTH), fixed2),
                  pl.BlockSpec((1, B_WIDTH), fixed2),
                  pl.BlockSpec((B_GROUPS, B_CHUNK, B_CHUNK), fixed3),
                  pl.BlockSpec((B_CHUNK, B_GROUPS), fixed2),
                  pl.BlockSpec((N_BRANCH, BRANCH_WIDTH, D_MODEL), fixed3),
                  pl.BlockSpec((D_MODEL, D_MODEL), fixed2),
                  pl.BlockSpec((1, D_MODEL), fixed2)],
        out_specs=pl.BlockSpec((tm, D_MODEL), row),
        out_shape=jax.ShapeDtypeStruct((seq, D_MODEL), F32),
        compiler_params=pltpu.CompilerParams(dimension_semantics=("arbitrary",),
                                             vmem_limit_bytes=VMEM_LIMIT),
        name="merge",
    )(x2, ya, yc, guv, zb, gates, ln_g.reshape(1, -1), ln_b.reshape(1, -1), sgu_w, sgu_b.T,
      wb_bf16, wo_bf16, final_g.reshape(1, -1))


def kernel(x, positions, norm_g, w_in, lam_q1, lam_k1, lam_q2, lam_k2, subln_g, sgu_ln_g, sgu_ln_b,
           sgu_w, sgu_b, w_branch, w_out, final_g):
    batch, seq, _ = x.shape
    depth = norm_g.shape[0]
    assert batch == 1 and seq % C_TILE == 0
    h = x.reshape(seq, D_MODEL)
    cos_t, sin_t = _rope_tables(positions.reshape(seq), seq)
    for l in range(depth):
        lam_init = 0.8 - 0.6 * math.exp(-0.3 * l)
        (qt, ka, vt, za, guv, zb, qc, kc, vc, zc, gates) = _inproj(
            h, norm_g[l], w_in[l].astype(BF16), cos_t, sin_t)
        ya = _attn_a(qt, ka, vt, za, subln_g[l], lam_q1[l], lam_k1[l], lam_q2[l], lam_k2[l], lam_init)
        yc = _attn_c(qc, kc, vc, zc)
        h = _merge(h, ya, yc, guv, zb, gates, sgu_ln_g[l], sgu_ln_b[l], sgu_w[l], sgu_b[l],
                   w_branch[l].astype(BF16), w_out[l].astype(BF16), final_g,
                   final_norm=(l == depth - 1))
    return h.reshape(batch, seq, D_MODEL)
```

```python
import functools
import math

import jax
import jax.numpy as jnp
import numpy as np
from jax import lax
from jax.experimental import pallas as pl
from jax.experimental.pallas import tpu as pltpu

D_MODEL = 1024
A_HEADS = 4
A_HEAD_DIM = 64
A_WIDTH = A_HEADS * 2 * A_HEAD_DIM
B_WIDTH = 512
B_GROUPS = 4
B_GROUP_DIM = B_WIDTH // B_GROUPS
B_CHUNK = 128
C_HEADS = 8
C_HEAD_DIM = 64
C_WIDTH = C_HEADS * C_HEAD_DIM
C_PATTERNS = ((128, 1), (512, 4), (2048, 16))
C_BLOCK = 128
N_BRANCH = 3
BRANCH_WIDTH = 512
ROPE_THETA = 500000.0
ROPE_FRAC = 4
EPS = 1e-6

LANES = 128
VMEM_LIMIT = 56 * 1024 * 1024

INPROJ_ROWS = 256
ATTN_A_BLOCK = 512
MERGE_ROWS = 512

_SIZES = (A_WIDTH, A_WIDTH, A_WIDTH, A_WIDTH, 2 * B_WIDTH, B_WIDTH,
          C_WIDTH, C_WIDTH, C_WIDTH, C_WIDTH, N_BRANCH * D_MODEL)
_OFFS = tuple(int(sum(_SIZES[:i])) for i in range(len(_SIZES)))
IN_COLS = int(sum(_SIZES))

F32 = jnp.float32
BF16 = jnp.bfloat16


def _silu(t):
    return t * (1.0 / (1.0 + jnp.exp(-t)))


def _sigmoid(t):
    return 1.0 / (1.0 + jnp.exp(-t))


def _gelu_tanh(t):
    c = math.sqrt(2.0 / math.pi)
    return 0.5 * t * (1.0 + jnp.tanh(c * (t + 0.044715 * (t * t * t))))


def _rope_table_kernel(pos_ref, inv_ref, sgn_ref, cos_ref, sin_ref):
    ang = pos_ref[...].astype(F32) * inv_ref[...]
    cos_ref[...] = jnp.cos(ang)
    sin_ref[...] = jnp.sin(ang) * sgn_ref[...]


def _rope_tables(positions, seq):
    rot = A_HEAD_DIM // ROPE_FRAC
    half = rot // 2
    inv = jnp.power(jnp.float32(ROPE_THETA), -jnp.arange(half, dtype=jnp.float32) * 2.0 / rot)
    zeros = jnp.zeros((A_HEAD_DIM - rot,), F32)
    inv_head = jnp.concatenate([inv, inv, zeros])
    sgn_head = jnp.concatenate([-jnp.ones((half,), F32), jnp.ones((half,), F32), zeros])
    reps = LANES // A_HEAD_DIM
    inv_lane = jnp.tile(inv_head, reps).reshape(1, LANES)
    sgn_lane = jnp.tile(sgn_head, reps).reshape(1, LANES)
    tm = 1024
    return pl.pallas_call(
        _rope_table_kernel,
        grid=(seq // tm,),
        in_specs=[pl.BlockSpec((tm, 1), lambda i: (i, 0)),
                  pl.BlockSpec((1, LANES), lambda i: (0, 0)),
                  pl.BlockSpec((1, LANES), lambda i: (0, 0))],
        out_specs=[pl.BlockSpec((tm, LANES), lambda i: (i, 0)),
                   pl.BlockSpec((tm, LANES), lambda i: (i, 0))],
        out_shape=[jax.ShapeDtypeStruct((seq, LANES), F32)] * 2,
        name="rope_tables",
    )(positions.reshape(seq, 1), inv_lane, sgn_lane)


def _inproj_kernel(x_ref, g_ref, w_ref, cos_ref, sin_ref,
                   qt_ref, ka_ref, vt_ref, za_ref, guv_ref, zb_ref,
                   qc_ref, kc_ref, vc_ref, zc_ref, gate_ref):
    x = x_ref[...]
    h = x * lax.rsqrt(jnp.mean(x * x, axis=-1, keepdims=True) + EPS) * g_ref[...]
    h = h.astype(BF16)
    cos = cos_ref[...]
    sin = sin_ref[...]
    tm = x.shape[0]
    lane = lax.broadcasted_iota(jnp.int32, (tm, LANES), 1)
    first_half = (lane % A_HEAD_DIM) < (A_HEAD_DIM // ROPE_FRAC // 2)
    shift = A_HEAD_DIM // ROPE_FRAC // 2

    def proj(lo, n):
        return jnp.dot(h, w_ref[:, lo:lo + n], preferred_element_type=F32)

    def rope_store(seg, out_ref, scale, width=A_WIDTH, transposed=False):
        full = proj(_OFFS[seg], width)
        for c in range(width // LANES):
            t = full[:, c * LANES:(c + 1) * LANES]
            partner = jnp.where(first_half, pltpu.roll(t, LANES - shift, 1), pltpu.roll(t, shift, 1))
            r = t * cos + partner * sin
            if scale != 1.0:
                r = r * scale
            if transposed:
                out_ref[c * LANES:(c + 1) * LANES, :] = r.T.astype(out_ref.dtype)
            else:
                out_ref[:, c * LANES:(c + 1) * LANES] = r.astype(out_ref.dtype)

    def act_store(seg, out_ref, fn, width=512):
        n = out_ref.shape[1]
        for c in range(n // width):
            t = proj(_OFFS[seg] + c * width, width)
            out_ref[:, c * width:(c + 1) * width] = fn(t).astype(out_ref.dtype)

    ident = lambda t: t
    rope_store(0, qt_ref, LOG2E / math.sqrt(A_HEAD_DIM), transposed=True)
    rope_store(1, ka_ref, 1.0)
    va = proj(_OFFS[2], A_WIDTH)
    dv = 2 * A_HEAD_DIM
    for hd in range(A_HEADS):
        vt_ref[hd, 0:dv, :] = va[:, hd * dv:(hd + 1) * dv].T.astype(vt_ref.dtype)
        vt_ref[hd, dv:dv + A_ONES_ROWS, :] = jnp.ones((A_ONES_ROWS, tm), vt_ref.dtype)
    act_store(3, za_ref, _silu)
    act_store(4, guv_ref, _gelu_tanh)
    act_store(5, zb_ref, _silu)
    rope_store(6, qc_ref, LOG2E / math.sqrt(C_HEAD_DIM))
    rope_store(7, kc_ref, 1.0)
    act_store(8, vc_ref, ident)
    act_store(9, zc_ref, _silu)
    act_store(10, gate_ref, _sigmoid)


def _inproj(x2, norm_g, w_bf16, cos_t, sin_t, tm=INPROJ_ROWS, tk=ATTN_A_BLOCK):
    seq = x2.shape[0]
    row = lambda i: (i, 0)
    fixed = lambda i: (0, 0)
    per_kv = tk // tm
    dv = 2 * A_HEAD_DIM + A_ONES_ROWS
    widths = (A_WIDTH, A_WIDTH, A_WIDTH, A_WIDTH, 2 * B_WIDTH, B_WIDTH,
              C_WIDTH, C_WIDTH, C_WIDTH, C_WIDTH, N_BRANCH * D_MODEL)
    dtypes = (BF16, BF16, BF16, BF16, BF16, BF16, F32, F32, F32, BF16, BF16)
    return pl.pallas_call(
        _inproj_kernel,
        grid=(seq // tm,),
        in_specs=[pl.BlockSpec((tm, D_MODEL), row),
                  pl.BlockSpec((1, D_MODEL), fixed),
                  pl.BlockSpec((D_MODEL, IN_COLS), fixed, pipeline_mode=pl.Buffered(1)),
                  pl.BlockSpec((tm, LANES), row),
                  pl.BlockSpec((tm, LANES), row)],
        out_specs=[pl.BlockSpec((A_WIDTH, tm), lambda i: (0, i)),
                   pl.BlockSpec((tm, A_WIDTH), row),
                   pl.BlockSpec((A_HEADS, None, dv, tm), lambda i: (0, i // per_kv, 0, i % per_kv))]
                  + [pl.BlockSpec((tm, w), row) for w in widths[3:]],
        out_shape=[jax.ShapeDtypeStruct((A_WIDTH, seq), BF16),
                   jax.ShapeDtypeStruct((seq, A_WIDTH), BF16),
                   jax.ShapeDtypeStruct((A_HEADS, seq // tk, dv, tk), BF16)]
                  + [jax.ShapeDtypeStruct((seq, w), d) for w, d in zip(widths[3:], dtypes[3:])],
        compiler_params=pltpu.CompilerParams(dimension_semantics=("arbitrary",),
                                             vmem_limit_bytes=VMEM_LIMIT),
        name="inproj",
    )(x2, norm_g.reshape(1, D_MODEL), w_bf16, cos_t, sin_t)


LOG2E = math.log2(math.e)
A_ONES_ROWS = 16
A_QCHUNK = 256


def _attn_a_kernel(qt_ref, k_ref, vt_ref, za_ref, g_ref, lq1_ref, lk1_ref, lq2_ref, lk2_ref,
                   o_ref, qz_scr, m_scr, acc_scr, s_a, s_b, p_a, p_b, al_a, al_b,
                   *, tq, tk, lam_init):
    i = pl.program_id(1)
    d = A_HEAD_DIM
    qt = qt_ref[...]
    row = lax.broadcasted_iota(jnp.int32, qt.shape, 0)
    zero = jnp.zeros_like(qt)
    q_maps = (jnp.where(row < d, qt, zero), jnp.where(row >= d, qt, zero))
    n_half = tq // A_QCHUNK
    for h in range(n_half):
        for m in range(2):
            qz_scr[m, h] = q_maps[m][:, h * A_QCHUNK:(h + 1) * A_QCHUNK]
    m_scr[...] = jnp.full(m_scr.shape, -jnp.inf, F32)
    acc_scr[...] = jnp.zeros(acc_scr.shape, F32)

    chunks = [(m, h) for h in range(n_half) for m in range(2)]

    def scores(j, s_buf, c):
        kb = k_ref[pl.ds(pl.multiple_of(j * tk, tk), tk), :]
        s_buf[c] = jnp.dot(kb, qz_scr[c], preferred_element_type=F32)

    def softmax(s_buf, p_buf, al_buf, masked, c):
        s = s_buf[c]
        if masked is not None:
            keyi = lax.broadcasted_iota(jnp.int32, s.shape, 0) + masked
            qryi = lax.broadcasted_iota(jnp.int32, s.shape, 1) + c[1] * A_QCHUNK
            s = jnp.where(keyi <= qryi, s, -jnp.inf)
        m_old = m_scr[c]
        m_new = jnp.maximum(m_old, jnp.max(s, axis=0, keepdims=True))
        p_buf[c] = jnp.exp2(s - m_new).astype(BF16)
        al_buf[c] = jnp.exp2(m_old - m_new)
        m_scr[c] = m_new

    def values(j, p_buf, al_buf, c):
        vtb = vt_ref[j]
        acc_scr[c] = al_buf[c] * acc_scr[c] + jnp.dot(vtb, p_buf[c], preferred_element_type=F32)

    def stage(val=None, sco=None, sm=None):
        for c in chunks:
            if sco is not None:
                scores(*sco, c)
            if val is not None:
                values(*val, c)
            if sm is not None:
                softmax(*sm, c)

    A = (p_a, al_a)
    B = (p_b, al_b)

    assert tq == 2 * tk
    last = 2 * i + 1
    stage(sco=(0, s_a))

    @pl.when(i == 0)
    def _():
        stage(sco=(1, s_b), sm=(s_a, *A, 0))

    @pl.when(i >= 1)
    def _():
        stage(sco=(1, s_b), sm=(s_a, *A, None))

    def pair(t):
        stage(sco=(t, s_a), val=(t - 2, *A), sm=(s_b, *B, None))
        stage(sco=(t + 1, s_b), val=(t - 1, *B), sm=(s_a, *A, None))

    def four_pairs(u, carry):
        for k in range(4):
            pair(2 + 8 * u + 2 * k)
        return carry

    n_pairs = jnp.maximum(i - 1, 0)
    n_quads = n_pairs // 4
    lax.fori_loop(0, n_quads, four_pairs, 0)

    def one_pair(u, carry):
        pair(2 + 8 * n_quads + 2 * u)
        return carry

    lax.fori_loop(0, n_pairs % 4, one_pair, 0)

    @pl.when(i >= 1)
    def _():
        stage(sco=(last - 1, s_a), val=(last - 3, *A), sm=(s_b, *B, None))
        stage(sco=(last, s_b), val=(last - 2, *B), sm=(s_a, *A, 0))

    stage(val=(last - 1, *A))
    stage(sm=(s_b, *B, tk))
    stage(val=(last, *B))

    lam = (jnp.exp(jnp.sum(lq1_ref[...] * lk1_ref[...], axis=-1, keepdims=True))
           - jnp.exp(jnp.sum(lq2_ref[...] * lk2_ref[...], axis=-1, keepdims=True)) + lam_init)
    dv = 2 * A_HEAD_DIM
    acc = [jnp.concatenate([acc_scr[m, h] for h in range(n_half)], axis=1) for m in range(2)]
    o0 = acc[0][0:dv] * (1.0 / acc[0][dv:dv + 1])
    o1 = acc[1][0:dv] * (1.0 / acc[1][dv:dv + 1])
    o = o0 - lam * o1
    y = o * lax.rsqrt(jnp.mean(o * o, axis=0, keepdims=True) + EPS) * g_ref[...]
    y = (y * (1.0 - lam_init)).T
    o_ref[...] = (y * za_ref[...].astype(F32)).astype(o_ref.dtype)


def _attn_a(qt, ka, vt, za, subln_g, lq1, lk1, lq2, lk2, lam_init, tq=2 * ATTN_A_BLOCK):
    seq = ka.shape[0]
    nk, tk = vt.shape[1], vt.shape[3]
    chunked = lambda rows, dtype: pltpu.VMEM((2, tq // A_QCHUNK, rows, A_QCHUNK), dtype)
    fixed = lambda h, i: (0, 0)
    vec = lambda a: a.reshape(1, -1)
    return pl.pallas_call(
        functools.partial(_attn_a_kernel, tq=tq, tk=tk, lam_init=lam_init),
        grid=(A_HEADS, seq // tq),
        in_specs=[pl.BlockSpec((LANES, tq), lambda h, i: (h, i)),
                  pl.BlockSpec((seq, LANES), lambda h, i: (0, h)),
                  pl.BlockSpec((None, nk, LANES + A_ONES_ROWS, tk), lambda h, i: (h, 0, 0, 0)),
                  pl.BlockSpec((tq, LANES), lambda h, i: (i, h)),
                  pl.BlockSpec((2 * A_HEAD_DIM, 1), fixed),
                  pl.BlockSpec((1, A_HEAD_DIM), fixed),
                  pl.BlockSpec((1, A_HEAD_DIM), fixed),
                  pl.BlockSpec((1, A_HEAD_DIM), fixed),
                  pl.BlockSpec((1, A_HEAD_DIM), fixed)],
        out_specs=pl.BlockSpec((tq, LANES), lambda h, i: (i, h)),
        out_shape=jax.ShapeDtypeStruct((seq, A_WIDTH), BF16),
        scratch_shapes=[chunked(LANES, BF16),
                        chunked(1, F32),
                        chunked(2 * A_HEAD_DIM + A_ONES_ROWS, F32),
                        chunked(tk, F32), chunked(tk, F32),
                        chunked(tk, BF16), chunked(tk, BF16),
                        chunked(1, F32), chunked(1, F32)],
        compiler_params=pltpu.CompilerParams(dimension_semantics=("arbitrary", "arbitrary"),
                                             vmem_limit_bytes=VMEM_LIMIT),
        name="attn_a",
    )(qt, ka, vt, za, subln_g.reshape(-1, 1), vec(lq1), vec(lk1), vec(lq2), vec(lk2))


C_TILE = C_BLOCK * max(dil for _, dil in C_PATTERNS)


def _attn_c_kernel(q_ref, kp_ref, kc_ref, vp_ref, vc_ref, z_ref, bias_ref, o_ref, out_scr, lse_scr):
    first_tile = (pl.program_id(1) == 0).astype(jnp.int32)
    head0 = lax.broadcasted_iota(jnp.int32, (LANES, C_BLOCK), 0) < C_HEAD_DIM

    def pick(t):
        return jnp.where(head0, t[:, :C_BLOCK], t[:, C_BLOCK:])

    def strided(start, dil):
        return pl.ds(start, C_BLOCK) if dil == 1 else pl.ds(start, C_BLOCK, stride=dil)

    for p, (window, dil) in enumerate(C_PATTERNS):
        assert window // dil == C_BLOCK
        span = C_BLOCK * dil
        n_blocks = C_TILE // C_BLOCK
        idx = []
        for b in range(n_blocks):
            u, r = divmod(b, dil)
            older = strided(C_TILE - span + r, dil) if u == 0 else strided((u - 1) * span + r, dil)
            idx.append((strided(u * span + r, dil), older, u == 0))

        def keys(prev_ref, cur_ref, q_rows, older, in_prev):
            return jnp.concatenate([(prev_ref if in_prev else cur_ref)[older, :], cur_ref[q_rows, :]], axis=0)

        q_both, v_t, s, e, stats, o_all = [], [], [], [], [], []
        for q_rows, older, in_prev in idx:
            qt = q_ref[q_rows, :].T
            zero = jnp.zeros_like(qt)
            q_both.append(jnp.concatenate([jnp.where(head0, qt, zero), jnp.where(head0, zero, qt)],
                                          axis=1).astype(BF16))
            v_t.append(keys(vp_ref, vc_ref, q_rows, older, in_prev).T.astype(BF16))
        for k, (q_rows, older, in_prev) in enumerate(idx):
            k2 = keys(kp_ref, kc_ref, q_rows, older, in_prev).astype(BF16)
            bias = bias_ref[first_tile] if in_prev else bias_ref[0]
            s.append(jnp.dot(k2, q_both[k], preferred_element_type=F32) + bias)
        for k in range(n_blocks):
            mx = jnp.max(s[k], axis=0, keepdims=True)
            ek = jnp.exp2(s[k] - mx)
            den = jnp.sum(ek, axis=0, keepdims=True)
            e.append(ek.astype(BF16))
            stats.append((1.0 / den, mx + jnp.log2(den)))
        for k in range(n_blocks):
            o_all.append(jnp.dot(v_t[k], e[k], preferred_element_type=F32))
        wide = (LANES, 2 * C_BLOCK)
        for k, (q_rows, _, _) in enumerate(idx):
            inv, lse = stats[k]
            o_t = pick(o_all[k]) * pick(jnp.broadcast_to(inv, wide))
            l_t = pick(jnp.broadcast_to(lse, wide))
            out_scr[p, q_rows, :] = o_t.T
            lse_scr[p, q_rows, :] = l_t.T

    l0, l1, l2 = lse_scr[0], lse_scr[1], lse_scr[2]
    mx = jnp.maximum(jnp.maximum(l0, l1), l2)
    w0 = jnp.exp2(l0 - mx)
    w1 = jnp.exp2(l1 - mx)
    w2 = jnp.exp2(l2 - mx)
    oc = (w0 * out_scr[0] + w1 * out_scr[1] + w2 * out_scr[2]) / (w0 + w1 + w2)
    o_ref[...] = (oc * z_ref[...].astype(F32)).astype(o_ref.dtype)


def _attn_c_bias():
    ki = np.arange(2 * C_BLOCK)[:, None]
    qi = np.tile(np.arange(C_BLOCK), LANES // C_HEAD_DIM)[None, :]
    rel = qi + C_BLOCK - ki
    band = (rel >= 0) & (rel <= C_BLOCK)
    masks = np.stack([band, band & (ki >= C_BLOCK)])
    return jnp.asarray(np.where(masks, 0.0, -np.inf), dtype=F32)


def _attn_c(qc, kc, vc, zc):
    seq = qc.shape[0]
    cur = lambda c, t: (t, c)
    prev = lambda c, t: (jnp.maximum(t - 1, 0), c)
    tile = (C_TILE, LANES)
    return pl.pallas_call(
        _attn_c_kernel,
        grid=(C_WIDTH // LANES, seq // C_TILE),
        in_specs=[pl.BlockSpec(tile, cur),
                  pl.BlockSpec(tile, prev), pl.BlockSpec(tile, cur),
                  pl.BlockSpec(tile, prev), pl.BlockSpec(tile, cur),
                  pl.BlockSpec(tile, cur),
                  pl.BlockSpec((2, 2 * C_BLOCK, 2 * C_BLOCK), lambda c, t: (0, 0, 0))],
        out_specs=pl.BlockSpec(tile, cur),
        out_shape=jax.ShapeDtypeStruct((seq, C_WIDTH), BF16),
        scratch_shapes=[pltpu.VMEM((len(C_PATTERNS), C_TILE, LANES), F32),
                        pltpu.VMEM((len(C_PATTERNS), C_TILE, LANES), F32)],
        compiler_params=pltpu.CompilerParams(dimension_semantics=("arbitrary", "arbitrary"),
                                             vmem_limit_bytes=VMEM_LIMIT),
        name="attn_c",
    )(qc, kc, kc, vc, vc, zc, _attn_c_bias())


def _merge_kernel(x_ref, ya_ref, yc_ref, guv_ref, zb_ref, gate_ref, lng_ref, lnb_ref,
                  sw_ref, sb_ref, wb_ref, wo_ref, fg_ref, o_ref, *, final_norm):
    tm = x_ref.shape[0]
    guv = guv_ref[...]
    u = guv[:, :B_WIDTH].astype(F32)
    vb = guv[:, B_WIDTH:].astype(F32)
    mu = jnp.mean(vb, axis=-1, keepdims=True)
    var = jnp.mean(jnp.square(vb - mu), axis=-1, keepdims=True)
    vb = ((vb - mu) * lax.rsqrt(var + EPS) * lng_ref[...] + lnb_ref[...]).astype(BF16)
    ti = lax.broadcasted_iota(jnp.int32, (B_CHUNK, B_CHUNK), 0)
    si = lax.broadcasted_iota(jnp.int32, (B_CHUNK, B_CHUNK), 1)
    causal = si <= ti
    rows = []
    for c in range(tm // B_CHUNK):
        cols = []
        for g in range(B_GROUPS):
            w = jnp.where(causal, sw_ref[g], 0.0).astype(BF16)
            blk = vb[c * B_CHUNK:(c + 1) * B_CHUNK, g * B_GROUP_DIM:(g + 1) * B_GROUP_DIM]
            mixed = jnp.dot(w, blk, preferred_element_type=F32) + sb_ref[:, g:g + 1]
            cols.append(mixed)
        rows.append(jnp.concatenate(cols, axis=-1))
    mixed = jnp.concatenate(rows, axis=0)
    yb = (u * mixed * zb_ref[...].astype(F32)).astype(BF16)

    gates = gate_ref[...]
    merged = jnp.zeros((tm, D_MODEL), F32)
    for n, y in enumerate((ya_ref[...], yb, yc_ref[...])):
        pb = jnp.dot(y, wb_ref[n], preferred_element_type=F32)
        merged = merged + gates[:, n * D_MODEL:(n + 1) * D_MODEL].astype(F32) * pb
    out = x_ref[...] + jnp.dot(merged.astype(BF16), wo_ref[...], preferred_element_type=F32)
    if final_norm:
        out = out * lax.rsqrt(jnp.mean(out * out, axis=-1, keepdims=True) + EPS) * fg_ref[...]
    o_ref[...] = out


def _merge(x2, ya, yc, guv, zb, gates, ln_g, ln_b, sgu_w, sgu_b, wb_bf16, wo_bf16, final_g,
           final_norm, tm=MERGE_ROWS):
    seq = x2.shape[0]
    row = lambda i: (i, 0)
    fixed2 = lambda i: (0, 0)
    fixed3 = lambda i: (0, 0, 0)
    return pl.pallas_call(
        functools.partial(_merge_kernel, final_norm=final_norm),
        grid=(seq // tm,),
        in_specs=[pl.BlockSpec((tm, D_MODEL), row),
                  pl.BlockSpec((tm, A_WIDTH), row),
                  pl.BlockSpec((tm, C_WIDTH), row),
                  pl.BlockSpec((tm, 2 * B_WIDTH), row),
                  pl.BlockSpec((tm, B_WIDTH), row),
                  pl.BlockSpec((tm, N_BRANCH * D_MODEL), row),
                  pl.BlockSpec((1, B_WIDTH), fixed2),
                  pl.BlockSpec((1, B_WIDTH), fixed2),
                  pl.BlockSpec((B_GROUPS, B_CHUNK, B_CHUNK), fixed3),
                  pl.BlockSpec((B_CHUNK, B_GROUPS), fixed2),
                  pl.BlockSpec((N_BRANCH, BRANCH_WIDTH, D_MODEL), fixed3),
                  pl.BlockSpec((D_MODEL, D_MODEL), fixed2),
                  pl.BlockSpec((1, D_MODEL), fixed2)],
        out_specs=pl.BlockSpec((tm, D_MODEL), row),
        out_shape=jax.ShapeDtypeStruct((seq, D_MODEL), F32),
        compiler_params=pltpu.CompilerParams(dimension_semantics=("arbitrary",),
                                             vmem_limit_bytes=VMEM_LIMIT),
        name="merge",
    )(x2, ya, yc, guv, zb, gates, ln_g.reshape(1, -1), ln_b.reshape(1, -1), sgu_w, sgu_b.T,
      wb_bf16, wo_bf16, final_g.reshape(1, -1))


def kernel(x, positions, norm_g, w_in, lam_q1, lam_k1, lam_q2, lam_k2, subln_g, sgu_ln_g, sgu_ln_b,
           sgu_w, sgu_b, w_branch, w_out, final_g):
    batch, seq, _ = x.shape
    depth = norm_g.shape[0]
    assert batch == 1 and seq % C_TILE == 0
    h = x.reshape(seq, D_MODEL)
    cos_t, sin_t = _rope_tables(positions.reshape(seq), seq)
    w_in, w_branch, w_out = w_in.astype(BF16), w_branch.astype(BF16), w_out.astype(BF16)
    for l in range(depth):
        lam_init = 0.8 - 0.6 * math.exp(-0.3 * l)
        (qt, ka, vt, za, guv, zb, qc, kc, vc, zc, gates) = _inproj(
            h, norm_g[l], w_in[l], cos_t, sin_t)
        ya = _attn_a(qt, ka, vt, za, subln_g[l], lam_q1[l], lam_k1[l], lam_q2[l], lam_k2[l], lam_init)
        yc = _attn_c(qc, kc, vc, zc)
        h = _merge(h, ya, yc, guv, zb, gates, sgu_ln_g[l], sgu_ln_b[l], sgu_w[l], sgu_b[l],
                   w_branch[l], w_out[l], final_g,
                   final_norm=(l == depth - 1))
    return h.reshape(batch, seq, D_MODEL)
```

```python
import functools
import math

import jax
import jax.numpy as jnp
import numpy as np
from jax import lax
from jax.experimental import pallas as pl
from jax.experimental.pallas import tpu as pltpu

D_MODEL = 1024
A_HEADS = 4
A_HEAD_DIM = 64
A_WIDTH = A_HEADS * 2 * A_HEAD_DIM
B_WIDTH = 512
B_GROUPS = 4
B_GROUP_DIM = B_WIDTH // B_GROUPS
B_CHUNK = 128
C_HEADS = 8
C_HEAD_DIM = 64
C_WIDTH = C_HEADS * C_HEAD_DIM
C_PATTERNS = ((128, 1), (512, 4), (2048, 16))
C_BLOCK = 128
N_BRANCH = 3
BRANCH_WIDTH = 512
ROPE_THETA = 500000.0
ROPE_FRAC = 4
EPS = 1e-6

LANES = 128
VMEM_LIMIT = 56 * 1024 * 1024

INPROJ_ROWS = 256
ATTN_A_BLOCK = 512
MERGE_ROWS = 512

_SIZES = (A_WIDTH, A_WIDTH, A_WIDTH, A_WIDTH, 2 * B_WIDTH, B_WIDTH,
          C_WIDTH, C_WIDTH, C_WIDTH, C_WIDTH, N_BRANCH * D_MODEL)
_OFFS = tuple(int(sum(_SIZES[:i])) for i in range(len(_SIZES)))
IN_COLS = int(sum(_SIZES))

F32 = jnp.float32
BF16 = jnp.bfloat16


def _silu(t):
    return t * (1.0 / (1.0 + jnp.exp(-t)))


def _sigmoid(t):
    return 1.0 / (1.0 + jnp.exp(-t))


def _gelu_tanh(t):
    c = math.sqrt(2.0 / math.pi)
    return 0.5 * t * (1.0 + jnp.tanh(c * (t + 0.044715 * (t * t * t))))


def _rope_table_kernel(pos_ref, inv_ref, sgn_ref, cos_ref, sin_ref):
    ang = pos_ref[...].astype(F32) * inv_ref[...]
    cos_ref[...] = jnp.cos(ang)
    sin_ref[...] = jnp.sin(ang) * sgn_ref[...]


def _rope_tables(positions, seq):
    rot = A_HEAD_DIM // ROPE_FRAC
    half = rot // 2
    inv = jnp.power(jnp.float32(ROPE_THETA), -jnp.arange(half, dtype=jnp.float32) * 2.0 / rot)
    zeros = jnp.zeros((A_HEAD_DIM - rot,), F32)
    inv_head = jnp.concatenate([inv, inv, zeros])
    sgn_head = jnp.concatenate([-jnp.ones((half,), F32), jnp.ones((half,), F32), zeros])
    reps = LANES // A_HEAD_DIM
    inv_lane = jnp.tile(inv_head, reps).reshape(1, LANES)
    sgn_lane = jnp.tile(sgn_head, reps).reshape(1, LANES)
    tm = 1024
    return pl.pallas_call(
        _rope_table_kernel,
        grid=(seq // tm,),
        in_specs=[pl.BlockSpec((tm, 1), lambda i: (i, 0)),
                  pl.BlockSpec((1, LANES), lambda i: (0, 0)),
                  pl.BlockSpec((1, LANES), lambda i: (0, 0))],
        out_specs=[pl.BlockSpec((tm, LANES), lambda i: (i, 0)),
                   pl.BlockSpec((tm, LANES), lambda i: (i, 0))],
        out_shape=[jax.ShapeDtypeStruct((seq, LANES), F32)] * 2,
        name="rope_tables",
    )(positions.reshape(seq, 1), inv_lane, sgn_lane)


def _inproj_kernel(x_ref, g_ref, w_ref, cos_ref, sin_ref,
                   qt_ref, ka_ref, vt_ref, za_ref, guv_ref, zb_ref,
                   qc_ref, kc_ref, vc_ref, zc_ref, gate_ref):
    x = x_ref[...]
    h = x * lax.rsqrt(jnp.mean(x * x, axis=-1, keepdims=True) + EPS) * g_ref[...]
    h = h.astype(BF16)
    cos = cos_ref[...]
    sin = sin_ref[...]
    tm = x.shape[0]
    lane = lax.broadcasted_iota(jnp.int32, (tm, LANES), 1)
    first_half = (lane % A_HEAD_DIM) < (A_HEAD_DIM // ROPE_FRAC // 2)
    shift = A_HEAD_DIM // ROPE_FRAC // 2

    def proj(lo, n):
        return jnp.dot(h, w_ref[:, lo:lo + n], preferred_element_type=F32)

    def rope_store(seg, out_ref, scale, width=A_WIDTH, transposed=False):
        full = proj(_OFFS[seg], width)
        for c in range(width // LANES):
            t = full[:, c * LANES:(c + 1) * LANES]
            partner = jnp.where(first_half, pltpu.roll(t, LANES - shift, 1), pltpu.roll(t, shift, 1))
            r = t * cos + partner * sin
            if scale != 1.0:
                r = r * scale
            if transposed:
                out_ref[c * LANES:(c + 1) * LANES, :] = r.T.astype(out_ref.dtype)
            else:
                out_ref[:, c * LANES:(c + 1) * LANES] = r.astype(out_ref.dtype)

    def act_store(seg, out_ref, fn, width=512):
        n = out_ref.shape[1]
        for c in range(n // width):
            t = proj(_OFFS[seg] + c * width, width)
            out_ref[:, c * width:(c + 1) * width] = fn(t).astype(out_ref.dtype)

    ident = lambda t: t
    rope_store(0, qt_ref, LOG2E / math.sqrt(A_HEAD_DIM), transposed=True)
    rope_store(1, ka_ref, 1.0)
    va = proj(_OFFS[2], A_WIDTH)
    dv = 2 * A_HEAD_DIM
    for hd in range(A_HEADS):
        vt_ref[hd, 0:dv, :] = va[:, hd * dv:(hd + 1) * dv].T.astype(vt_ref.dtype)
        vt_ref[hd, dv:dv + A_ONES_ROWS, :] = jnp.ones((A_ONES_ROWS, tm), vt_ref.dtype)
    act_store(3, za_ref, _silu)
    act_store(4, guv_ref, _gelu_tanh)
    act_store(5, zb_ref, _silu)
    rope_store(6, qc_ref, LOG2E / math.sqrt(C_HEAD_DIM))
    rope_store(7, kc_ref, 1.0)
    act_store(8, vc_ref, ident)
    act_store(9, zc_ref, _silu)
    act_store(10, gate_ref, _sigmoid)


def _inproj(x2, norm_g, w_bf16, layer, cos_t, sin_t, tm=INPROJ_ROWS, tk=ATTN_A_BLOCK):
    seq = x2.shape[0]
    row = lambda i: (i, 0)
    fixed = lambda i: (0, 0)
    per_kv = tk // tm
    dv = 2 * A_HEAD_DIM + A_ONES_ROWS
    widths = (A_WIDTH, A_WIDTH, A_WIDTH, A_WIDTH, 2 * B_WIDTH, B_WIDTH,
              C_WIDTH, C_WIDTH, C_WIDTH, C_WIDTH, N_BRANCH * D_MODEL)
    dtypes = (BF16, BF16, BF16, BF16, BF16, BF16, F32, F32, F32, BF16, BF16)
    return pl.pallas_call(
        _inproj_kernel,
        grid=(seq // tm,),
        in_specs=[pl.BlockSpec((tm, D_MODEL), row),
                  pl.BlockSpec((1, D_MODEL), fixed),
                  pl.BlockSpec((None, D_MODEL, IN_COLS), lambda i: (layer, 0, 0),
                               pipeline_mode=pl.Buffered(1)),
                  pl.BlockSpec((tm, LANES), row),
                  pl.BlockSpec((tm, LANES), row)],
        out_specs=[pl.BlockSpec((A_WIDTH, tm), lambda i: (0, i)),
                   pl.BlockSpec((tm, A_WIDTH), row),
                   pl.BlockSpec((A_HEADS, None, dv, tm), lambda i: (0, i // per_kv, 0, i % per_kv))]
                  + [pl.BlockSpec((tm, w), row) for w in widths[3:]],
        out_shape=[jax.ShapeDtypeStruct((A_WIDTH, seq), BF16),
                   jax.ShapeDtypeStruct((seq, A_WIDTH), BF16),
                   jax.ShapeDtypeStruct((A_HEADS, seq // tk, dv, tk), BF16)]
                  + [jax.ShapeDtypeStruct((seq, w), d) for w, d in zip(widths[3:], dtypes[3:])],
        compiler_params=pltpu.CompilerParams(dimension_semantics=("arbitrary",),
                                             vmem_limit_bytes=VMEM_LIMIT),
        name="inproj",
    )(x2, norm_g.reshape(1, D_MODEL), w_bf16, cos_t, sin_t)


LOG2E = math.log2(math.e)
A_ONES_ROWS = 16
A_QCHUNK = 256


def _attn_a_kernel(qt_ref, k_ref, vt_ref, za_ref, g_ref, lq1_ref, lk1_ref, lq2_ref, lk2_ref,
                   o_ref, qz_scr, m_scr, acc_scr, s_a, s_b, p_a, p_b, al_a, al_b,
                   *, tq, tk, lam_init):
    i = pl.program_id(1)
    d = A_HEAD_DIM
    qt = qt_ref[...]
    row = lax.broadcasted_iota(jnp.int32, qt.shape, 0)
    zero = jnp.zeros_like(qt)
    q_maps = (jnp.where(row < d, qt, zero), jnp.where(row >= d, qt, zero))
    n_half = tq // A_QCHUNK
    for h in range(n_half):
        for m in range(2):
            qz_scr[m, h] = q_maps[m][:, h * A_QCHUNK:(h + 1) * A_QCHUNK]
    m_scr[...] = jnp.full(m_scr.shape, -jnp.inf, F32)
    acc_scr[...] = jnp.zeros(acc_scr.shape, F32)

    chunks = [(m, h) for h in range(n_half) for m in range(2)]

    def scores(j, s_buf, c):
        kb = k_ref[pl.ds(pl.multiple_of(j * tk, tk), tk), :]
        s_buf[c] = jnp.dot(kb, qz_scr[c], preferred_element_type=F32)

    def softmax(s_buf, p_buf, al_buf, masked, c):
        s = s_buf[c]
        if masked is not None:
            keyi = lax.broadcasted_iota(jnp.int32, s.shape, 0) + masked
            qryi = lax.broadcasted_iota(jnp.int32, s.shape, 1) + c[1] * A_QCHUNK
            s = jnp.where(keyi <= qryi, s, -jnp.inf)
        m_old = m_scr[c]
        m_new = jnp.maximum(m_old, jnp.max(s, axis=0, keepdims=True))
        p_buf[c] = jnp.exp2(s - m_new).astype(BF16)
        al_buf[c] = jnp.exp2(m_old - m_new)
        m_scr[c] = m_new

    def values(j, p_buf, al_buf, c):
        vtb = vt_ref[j]
        acc_scr[c] = al_buf[c] * acc_scr[c] + jnp.dot(vtb, p_buf[c], preferred_element_type=F32)

    def stage(val=None, sco=None, sm=None):
        for c in chunks:
            if sco is not None:
                scores(*sco, c)
            if val is not None:
                values(*val, c)
            if sm is not None:
                softmax(*sm, c)

    A = (p_a, al_a)
    B = (p_b, al_b)

    assert tq == 2 * tk
    last = 2 * i + 1
    stage(sco=(0, s_a))

    @pl.when(i == 0)
    def _():
        stage(sco=(1, s_b), sm=(s_a, *A, 0))

    @pl.when(i >= 1)
    def _():
        stage(sco=(1, s_b), sm=(s_a, *A, None))

    def pair(t):
        stage(sco=(t, s_a), val=(t - 2, *A), sm=(s_b, *B, None))
        stage(sco=(t + 1, s_b), val=(t - 1, *B), sm=(s_a, *A, None))

    def four_pairs(u, carry):
        for k in range(4):
            pair(2 + 8 * u + 2 * k)
        return carry

    n_pairs = jnp.maximum(i - 1, 0)
    n_quads = n_pairs // 4
    lax.fori_loop(0, n_quads, four_pairs, 0)

    def one_pair(u, carry):
        pair(2 + 8 * n_quads + 2 * u)
        return carry

    lax.fori_loop(0, n_pairs % 4, one_pair, 0)

    @pl.when(i >= 1)
    def _():
        stage(sco=(last - 1, s_a), val=(last - 3, *A), sm=(s_b, *B, None))
        stage(sco=(last, s_b), val=(last - 2, *B), sm=(s_a, *A, 0))

    stage(val=(last - 1, *A))
    stage(sm=(s_b, *B, tk))
    stage(val=(last, *B))

    lam = (jnp.exp(jnp.sum(lq1_ref[...] * lk1_ref[...], axis=-1, keepdims=True))
           - jnp.exp(jnp.sum(lq2_ref[...] * lk2_ref[...], axis=-1, keepdims=True)) + lam_init)
    dv = 2 * A_HEAD_DIM
    acc = [jnp.concatenate([acc_scr[m, h] for h in range(n_half)], axis=1) for m in range(2)]
    o0 = acc[0][0:dv] * (1.0 / acc[0][dv:dv + 1])
    o1 = acc[1][0:dv] * (1.0 / acc[1][dv:dv + 1])
    o = o0 - lam * o1
    y = o * lax.rsqrt(jnp.mean(o * o, axis=0, keepdims=True) + EPS) * g_ref[...]
    y = (y * (1.0 - lam_init)).T
    o_ref[...] = (y * za_ref[...].astype(F32)).astype(o_ref.dtype)


def _attn_a(qt, ka, vt, za, subln_g, lq1, lk1, lq2, lk2, lam_init, tq=2 * ATTN_A_BLOCK):
    seq = ka.shape[0]
    nk, tk = vt.shape[1], vt.shape[3]
    chunked = lambda rows, dtype: pltpu.VMEM((2, tq // A_QCHUNK, rows, A_QCHUNK), dtype)
    fixed = lambda h, i: (0, 0)
    vec = lambda a: a.reshape(1, -1)
    return pl.pallas_call(
        functools.partial(_attn_a_kernel, tq=tq, tk=tk, lam_init=lam_init),
        grid=(A_HEADS, seq // tq),
        in_specs=[pl.BlockSpec((LANES, tq), lambda h, i: (h, i)),
                  pl.BlockSpec((seq, LANES), lambda h, i: (0, h)),
                  pl.BlockSpec((None, nk, LANES + A_ONES_ROWS, tk), lambda h, i: (h, 0, 0, 0)),
                  pl.BlockSpec((tq, LANES), lambda h, i: (i, h)),
                  pl.BlockSpec((2 * A_HEAD_DIM, 1), fixed),
                  pl.BlockSpec((1, A_HEAD_DIM), fixed),
                  pl.BlockSpec((1, A_HEAD_DIM), fixed),
                  pl.BlockSpec((1, A_HEAD_DIM), fixed),
                  pl.BlockSpec((1, A_HEAD_DIM), fixed)],
        out_specs=pl.BlockSpec((tq, LANES), lambda h, i: (i, h)),
        out_shape=jax.ShapeDtypeStruct((seq, A_WIDTH), BF16),
        scratch_shapes=[chunked(LANES, BF16),
                        chunked(1, F32),
                        chunked(2 * A_HEAD_DIM + A_ONES_ROWS, F32),
                        chunked(tk, F32), chunked(tk, F32),
                        chunked(tk, BF16), chunked(tk, BF16),
                        chunked(1, F32), chunked(1, F32)],
        compiler_params=pltpu.CompilerParams(dimension_semantics=("arbitrary", "arbitrary"),
                                             vmem_limit_bytes=VMEM_LIMIT),
        name="attn_a",
    )(qt, ka, vt, za, subln_g.reshape(-1, 1), vec(lq1), vec(lk1), vec(lq2), vec(lk2))


C_TILE = C_BLOCK * max(dil for _, dil in C_PATTERNS)


def _attn_c_kernel(q_ref, kp_ref, kc_ref, vp_ref, vc_ref, z_ref, bias_ref, o_ref, out_scr, lse_scr):
    first_tile = (pl.program_id(1) == 0).astype(jnp.int32)
    head0 = lax.broadcasted_iota(jnp.int32, (LANES, C_BLOCK), 0) < C_HEAD_DIM

    def pick(t):
        return jnp.where(head0, t[:, :C_BLOCK], t[:, C_BLOCK:])

    def strided(start, dil):
        return pl.ds(start, C_BLOCK) if dil == 1 else pl.ds(start, C_BLOCK, stride=dil)

    for p, (window, dil) in enumerate(C_PATTERNS):
        assert window // dil == C_BLOCK
        span = C_BLOCK * dil
        n_blocks = C_TILE // C_BLOCK
        idx = []
        for b in range(n_blocks):
            u, r = divmod(b, dil)
            older = strided(C_TILE - span + r, dil) if u == 0 else strided((u - 1) * span + r, dil)
            idx.append((strided(u * span + r, dil), older, u == 0))

        def keys(prev_ref, cur_ref, q_rows, older, in_prev):
            return jnp.concatenate([(prev_ref if in_prev else cur_ref)[older, :], cur_ref[q_rows, :]], axis=0)

        q_both, v_t, s, e, stats, o_all = [], [], [], [], [], []
        for q_rows, older, in_prev in idx:
            qt = q_ref[q_rows, :].T
            zero = jnp.zeros_like(qt)
            q_both.append(jnp.concatenate([jnp.where(head0, qt, zero), jnp.where(head0, zero, qt)],
                                          axis=1).astype(BF16))
            v_t.append(keys(vp_ref, vc_ref, q_rows, older, in_prev).T.astype(BF16))
        for k, (q_rows, older, in_prev) in enumerate(idx):
            k2 = keys(kp_ref, kc_ref, q_rows, older, in_prev).astype(BF16)
            bias = bias_ref[first_tile] if in_prev else bias_ref[0]
            s.append(jnp.dot(k2, q_both[k], preferred_element_type=F32) + bias)
        for k in range(n_blocks):
            mx = jnp.max(s[k], axis=0, keepdims=True)
            ek = jnp.exp2(s[k] - mx)
            den = jnp.sum(ek, axis=0, keepdims=True)
            e.append(ek.astype(BF16))
            stats.append((1.0 / den, mx + jnp.log2(den)))
        for k in range(n_blocks):
            o_all.append(jnp.dot(v_t[k], e[k], preferred_element_type=F32))
        wide = (LANES, 2 * C_BLOCK)
        for k, (q_rows, _, _) in enumerate(idx):
            inv, lse = stats[k]
            o_t = pick(o_all[k]) * pick(jnp.broadcast_to(inv, wide))
            l_t = pick(jnp.broadcast_to(lse, wide))
            out_scr[p, q_rows, :] = o_t.T
            lse_scr[p, q_rows, :] = l_t.T

    l0, l1, l2 = lse_scr[0], lse_scr[1], lse_scr[2]
    mx = jnp.maximum(jnp.maximum(l0, l1), l2)
    w0 = jnp.exp2(l0 - mx)
    w1 = jnp.exp2(l1 - mx)
    w2 = jnp.exp2(l2 - mx)
    oc = (w0 * out_scr[0] + w1 * out_scr[1] + w2 * out_scr[2]) / (w0 + w1 + w2)
    o_ref[...] = (oc * z_ref[...].astype(F32)).astype(o_ref.dtype)


def _attn_c_bias():
    ki = np.arange(2 * C_BLOCK)[:, None]
    qi = np.tile(np.arange(C_BLOCK), LANES // C_HEAD_DIM)[None, :]
    rel = qi + C_BLOCK - ki
    band = (rel >= 0) & (rel <= C_BLOCK)
    masks = np.stack([band, band & (ki >= C_BLOCK)])
    return jnp.asarray(np.where(masks, 0.0, -np.inf), dtype=F32)


def _attn_c(qc, kc, vc, zc):
    seq = qc.shape[0]
    cur = lambda c, t: (t, c)
    prev = lambda c, t: (jnp.maximum(t - 1, 0), c)
    tile = (C_TILE, LANES)
    return pl.pallas_call(
        _attn_c_kernel,
        grid=(C_WIDTH // LANES, seq // C_TILE),
        in_specs=[pl.BlockSpec(tile, cur),
                  pl.BlockSpec(tile, prev), pl.BlockSpec(tile, cur),
                  pl.BlockSpec(tile, prev), pl.BlockSpec(tile, cur),
                  pl.BlockSpec(tile, cur),
                  pl.BlockSpec((2, 2 * C_BLOCK, 2 * C_BLOCK), lambda c, t: (0, 0, 0))],
        out_specs=pl.BlockSpec(tile, cur),
        out_shape=jax.ShapeDtypeStruct((seq, C_WIDTH), BF16),
        scratch_shapes=[pltpu.VMEM((len(C_PATTERNS), C_TILE, LANES), F32),
                        pltpu.VMEM((len(C_PATTERNS), C_TILE, LANES), F32)],
        compiler_params=pltpu.CompilerParams(dimension_semantics=("arbitrary", "arbitrary"),
                                             vmem_limit_bytes=VMEM_LIMIT),
        name="attn_c",
    )(qc, kc, kc, vc, vc, zc, _attn_c_bias())


def _merge_kernel(x_ref, ya_ref, yc_ref, guv_ref, zb_ref, gate_ref, lng_ref, lnb_ref,
                  sw_ref, sb_ref, wb_ref, wo_ref, fg_ref, o_ref, *, final_norm):
    tm = x_ref.shape[0]
    guv = guv_ref[...]
    u = guv[:, :B_WIDTH].astype(F32)
    vb = guv[:, B_WIDTH:].astype(F32)
    mu = jnp.mean(vb, axis=-1, keepdims=True)
    var = jnp.mean(jnp.square(vb - mu), axis=-1, keepdims=True)
    vb = ((vb - mu) * lax.rsqrt(var + EPS) * lng_ref[...] + lnb_ref[...]).astype(BF16)
    ti = lax.broadcasted_iota(jnp.int32, (B_CHUNK, B_CHUNK), 0)
    si = lax.broadcasted_iota(jnp.int32, (B_CHUNK, B_CHUNK), 1)
    causal = si <= ti
    rows = []
    for c in range(tm // B_CHUNK):
        cols = []
        for g in range(B_GROUPS):
            w = jnp.where(causal, sw_ref[g], 0.0).astype(BF16)
            blk = vb[c * B_CHUNK:(c + 1) * B_CHUNK, g * B_GROUP_DIM:(g + 1) * B_GROUP_DIM]
            mixed = jnp.dot(w, blk, preferred_element_type=F32) + sb_ref[:, g:g + 1]
            cols.append(mixed)
        rows.append(jnp.concatenate(cols, axis=-1))
    mixed = jnp.concatenate(rows, axis=0)
    yb = (u * mixed * zb_ref[...].astype(F32)).astype(BF16)

    gates = gate_ref[...]
    merged = jnp.zeros((tm, D_MODEL), F32)
    for n, y in enumerate((ya_ref[...], yb, yc_ref[...])):
        pb = jnp.dot(y, wb_ref[n], preferred_element_type=F32)
        merged = merged + gates[:, n * D_MODEL:(n + 1) * D_MODEL].astype(F32) * pb
    out = x_ref[...] + jnp.dot(merged.astype(BF16), wo_ref[...], preferred_element_type=F32)
    if final_norm:
        out = out * lax.rsqrt(jnp.mean(out * out, axis=-1, keepdims=True) + EPS) * fg_ref[...]
    o_ref[...] = out


def _merge(x2, ya, yc, guv, zb, gates, ln_g, ln_b, sgu_w, sgu_b, wb_bf16, wo_bf16, layer, final_g,
           final_norm, tm=MERGE_ROWS):
    seq = x2.shape[0]
    row = lambda i: (i, 0)
    fixed2 = lambda i: (0, 0)
    fixed3 = lambda i: (0, 0, 0)
    return pl.pallas_call(
        functools.partial(_merge_kernel, final_norm=final_norm),
        grid=(seq // tm,),
        in_specs=[pl.BlockSpec((tm, D_MODEL), row),
                  pl.BlockSpec((tm, A_WIDTH), row),
                  pl.BlockSpec((tm, C_WIDTH), row),
                  pl.BlockSpec((tm, 2 * B_WIDTH), row),
                  pl.BlockSpec((tm, B_WIDTH), row),
                  pl.BlockSpec((tm, N_BRANCH * D_MODEL), row),
                  pl.BlockSpec((1, B_WIDTH), fixed2),
                  pl.BlockSpec((1, B_WIDTH), fixed2),
                  pl.BlockSpec((B_GROUPS, B_CHUNK, B_CHUNK), fixed3),
                  pl.BlockSpec((B_CHUNK, B_GROUPS), fixed2),
                  pl.BlockSpec((None, N_BRANCH, BRANCH_WIDTH, D_MODEL), lambda i: (layer, 0, 0, 0)),
                  pl.BlockSpec((None, D_MODEL, D_MODEL), lambda i: (layer, 0, 0)),
                  pl.BlockSpec((1, D_MODEL), fixed2)],
        out_specs=pl.BlockSpec((tm, D_MODEL), row),
        out_shape=jax.ShapeDtypeStruct((seq, D_MODEL), F32),
        compiler_params=pltpu.CompilerParams(dimension_semantics=("arbitrary",),
                                             vmem_limit_bytes=VMEM_LIMIT),
        name="merge",
    )(x2, ya, yc, guv, zb, gates, ln_g.reshape(1, -1), ln_b.reshape(1, -1), sgu_w, sgu_b.T,
      wb_bf16, wo_bf16, final_g.reshape(1, -1))


def kernel(x, positions, norm_g, w_in, lam_q1, lam_k1, lam_q2, lam_k2, subln_g, sgu_ln_g, sgu_ln_b,
           sgu_w, sgu_b, w_branch, w_out, final_g):
    batch, seq, _ = x.shape
    depth = norm_g.shape[0]
    assert batch == 1 and seq % C_TILE == 0
    h = x.reshape(seq, D_MODEL)
    cos_t, sin_t = _rope_tables(positions.reshape(seq), seq)
    w_in, w_branch, w_out = w_in.astype(BF16), w_branch.astype(BF16), w_out.astype(BF16)
    for l in range(depth):
        lam_init = 0.8 - 0.6 * math.exp(-0.3 * l)
        (qt, ka, vt, za, guv, zb, qc, kc, vc, zc, gates) = _inproj(
            h, norm_g[l], w_in, l, cos_t, sin_t)
        ya = _attn_a(qt, ka, vt, za, subln_g[l], lam_q1[l], lam_k1[l], lam_q2[l], lam_k2[l], lam_init)
        yc = _attn_c(qc, kc, vc, zc)
        h = _merge(h, ya, yc, guv, zb, gates, sgu_ln_g[l], sgu_ln_b[l], sgu_w[l], sgu_b[l],
                   w_branch, w_out, l, final_g,
                   final_norm=(l == depth - 1))
    return h.reshape(batch, seq, D_MODEL)
```

```python
import functools
import math

import jax
import jax.numpy as jnp
import numpy as np
from jax import lax
from jax.experimental import pallas as pl
from jax.experimental.pallas import tpu as pltpu

D_MODEL = 1024
A_HEADS = 4
A_HEAD_DIM = 64
A_WIDTH = A_HEADS * 2 * A_HEAD_DIM
B_WIDTH = 512
B_GROUPS = 4
B_GROUP_DIM = B_WIDTH // B_GROUPS
B_CHUNK = 128
C_HEADS = 8
C_HEAD_DIM = 64
C_WIDTH = C_HEADS * C_HEAD_DIM
C_PATTERNS = ((128, 1), (512, 4), (2048, 16))
C_BLOCK = 128
N_BRANCH = 3
BRANCH_WIDTH = 512
ROPE_THETA = 500000.0
ROPE_FRAC = 4
EPS = 1e-6

LANES = 128
VMEM_LIMIT = 56 * 1024 * 1024

INPROJ_ROWS = 256
ATTN_A_BLOCK = 512
MERGE_ROWS = 512

_SIZES = (A_WIDTH, A_WIDTH, A_WIDTH, A_WIDTH, 2 * B_WIDTH, B_WIDTH,
          C_WIDTH, C_WIDTH, C_WIDTH, C_WIDTH, N_BRANCH * D_MODEL)
_OFFS = tuple(int(sum(_SIZES[:i])) for i in range(len(_SIZES)))
IN_COLS = int(sum(_SIZES))

F32 = jnp.float32
BF16 = jnp.bfloat16


def _silu(t):
    return t * (1.0 / (1.0 + jnp.exp(-t)))


def _sigmoid(t):
    return 1.0 / (1.0 + jnp.exp(-t))


def _gelu_tanh(t):
    c = math.sqrt(2.0 / math.pi)
    return 0.5 * t * (1.0 + jnp.tanh(c * (t + 0.044715 * (t * t * t))))


def _rope_table_kernel(pos_ref, inv_ref, sgn_ref, cos_ref, sin_ref):
    ang = pos_ref[...].astype(F32) * inv_ref[...]
    cos_ref[...] = jnp.cos(ang)
    sin_ref[...] = jnp.sin(ang) * sgn_ref[...]


def _rope_tables(positions, seq):
    rot = A_HEAD_DIM // ROPE_FRAC
    half = rot // 2
    inv = jnp.power(jnp.float32(ROPE_THETA), -jnp.arange(half, dtype=jnp.float32) * 2.0 / rot)
    zeros = jnp.zeros((A_HEAD_DIM - rot,), F32)
    inv_head = jnp.concatenate([inv, inv, zeros])
    sgn_head = jnp.concatenate([-jnp.ones((half,), F32), jnp.ones((half,), F32), zeros])
    reps = LANES // A_HEAD_DIM
    inv_lane = jnp.tile(inv_head, reps).reshape(1, LANES)
    sgn_lane = jnp.tile(sgn_head, reps).reshape(1, LANES)
    tm = 1024
    return pl.pallas_call(
        _rope_table_kernel,
        grid=(seq // tm,),
        in_specs=[pl.BlockSpec((tm, 1), lambda i: (i, 0)),
                  pl.BlockSpec((1, LANES), lambda i: (0, 0)),
                  pl.BlockSpec((1, LANES), lambda i: (0, 0))],
        out_specs=[pl.BlockSpec((tm, LANES), lambda i: (i, 0)),
                   pl.BlockSpec((tm, LANES), lambda i: (i, 0))],
        out_shape=[jax.ShapeDtypeStruct((seq, LANES), F32)] * 2,
        name="rope_tables",
    )(positions.reshape(seq, 1), inv_lane, sgn_lane)


def _inproj_kernel(x_ref, g_ref, w_ref, cos_ref, sin_ref,
                   qt_ref, ka_ref, vt_ref, za_ref, guv_ref, zb_ref,
                   qc_ref, kc_ref, vc_ref, zc_ref):
    x = x_ref[...]
    h = x * lax.rsqrt(jnp.mean(x * x, axis=-1, keepdims=True) + EPS) * g_ref[...]
    h = h.astype(BF16)
    cos = cos_ref[...]
    sin = sin_ref[...]
    tm = x.shape[0]
    lane = lax.broadcasted_iota(jnp.int32, (tm, LANES), 1)
    first_half = (lane % A_HEAD_DIM) < (A_HEAD_DIM // ROPE_FRAC // 2)
    shift = A_HEAD_DIM // ROPE_FRAC // 2

    def proj(lo, n):
        return jnp.dot(h, w_ref[:, lo:lo + n], preferred_element_type=F32)

    def rope_store(seg, out_ref, scale, width=A_WIDTH, transposed=False):
        full = proj(_OFFS[seg], width)
        for c in range(width // LANES):
            t = full[:, c * LANES:(c + 1) * LANES]
            partner = jnp.where(first_half, pltpu.roll(t, LANES - shift, 1), pltpu.roll(t, shift, 1))
            r = t * cos + partner * sin
            if scale != 1.0:
                r = r * scale
            if transposed:
                out_ref[c * LANES:(c + 1) * LANES, :] = r.T.astype(out_ref.dtype)
            else:
                out_ref[:, c * LANES:(c + 1) * LANES] = r.astype(out_ref.dtype)

    def act_store(seg, out_ref, fn, width=512):
        n = out_ref.shape[1]
        for c in range(n // width):
            t = proj(_OFFS[seg] + c * width, width)
            out_ref[:, c * width:(c + 1) * width] = fn(t).astype(out_ref.dtype)

    ident = lambda t: t
    rope_store(0, qt_ref, LOG2E / math.sqrt(A_HEAD_DIM), transposed=True)
    rope_store(1, ka_ref, 1.0)
    va = proj(_OFFS[2], A_WIDTH)
    dv = 2 * A_HEAD_DIM
    for hd in range(A_HEADS):
        vt_ref[hd, 0:dv, :] = va[:, hd * dv:(hd + 1) * dv].T.astype(vt_ref.dtype)
        vt_ref[hd, dv:dv + A_ONES_ROWS, :] = jnp.ones((A_ONES_ROWS, tm), vt_ref.dtype)
    act_store(3, za_ref, _silu)
    act_store(4, guv_ref, _gelu_tanh)
    act_store(5, zb_ref, _silu)
    rope_store(6, qc_ref, LOG2E / math.sqrt(C_HEAD_DIM))
    rope_store(7, kc_ref, 1.0)
    act_store(8, vc_ref, ident)
    act_store(9, zc_ref, _silu)


def _inproj(x2, norm_g, w_bf16, layer, cos_t, sin_t, tm=INPROJ_ROWS, tk=ATTN_A_BLOCK):
    seq = x2.shape[0]
    row = lambda i: (i, 0)
    fixed = lambda i: (0, 0)
    per_kv = tk // tm
    dv = 2 * A_HEAD_DIM + A_ONES_ROWS
    widths = (A_WIDTH, A_WIDTH, A_WIDTH, A_WIDTH, 2 * B_WIDTH, B_WIDTH,
              C_WIDTH, C_WIDTH, C_WIDTH, C_WIDTH, N_BRANCH * D_MODEL)
    dtypes = (BF16, BF16, BF16, BF16, BF16, BF16, F32, F32, F32, BF16, BF16)
    return pl.pallas_call(
        _inproj_kernel,
        grid=(seq // tm,),
        in_specs=[pl.BlockSpec((tm, D_MODEL), row),
                  pl.BlockSpec((1, D_MODEL), fixed),
                  pl.BlockSpec((None, D_MODEL, IN_COLS), lambda i: (layer, 0, 0),
                               pipeline_mode=pl.Buffered(1)),
                  pl.BlockSpec((tm, LANES), row),
                  pl.BlockSpec((tm, LANES), row)],
        out_specs=[pl.BlockSpec((A_WIDTH, tm), lambda i: (0, i)),
                   pl.BlockSpec((tm, A_WIDTH), row),
                   pl.BlockSpec((A_HEADS, None, dv, tm), lambda i: (0, i // per_kv, 0, i % per_kv))]
                  + [pl.BlockSpec((tm, w), row) for w in widths[3:10]],
        out_shape=[jax.ShapeDtypeStruct((A_WIDTH, seq), BF16),
                   jax.ShapeDtypeStruct((seq, A_WIDTH), BF16),
                   jax.ShapeDtypeStruct((A_HEADS, seq // tk, dv, tk), BF16)]
                  + [jax.ShapeDtypeStruct((seq, w), d) for w, d in zip(widths[3:10], dtypes[3:10])],
        compiler_params=pltpu.CompilerParams(dimension_semantics=("arbitrary",),
                                             vmem_limit_bytes=VMEM_LIMIT),
        name="inproj",
    )(x2, norm_g.reshape(1, D_MODEL), w_bf16, cos_t, sin_t)


LOG2E = math.log2(math.e)
A_ONES_ROWS = 16
A_QCHUNK = 256


def _attn_a_kernel(qt_ref, k_ref, vt_ref, za_ref, g_ref, lq1_ref, lk1_ref, lq2_ref, lk2_ref,
                   o_ref, qz_scr, m_scr, acc_scr, s_a, s_b, p_a, p_b, al_a, al_b,
                   *, tq, tk, lam_init):
    i = pl.program_id(1)
    d = A_HEAD_DIM
    qt = qt_ref[...]
    row = lax.broadcasted_iota(jnp.int32, qt.shape, 0)
    zero = jnp.zeros_like(qt)
    q_maps = (jnp.where(row < d, qt, zero), jnp.where(row >= d, qt, zero))
    n_half = tq // A_QCHUNK
    for h in range(n_half):
        for m in range(2):
            qz_scr[m, h] = q_maps[m][:, h * A_QCHUNK:(h + 1) * A_QCHUNK]
    m_scr[...] = jnp.full(m_scr.shape, -jnp.inf, F32)
    acc_scr[...] = jnp.zeros(acc_scr.shape, F32)

    chunks = [(m, h) for h in range(n_half) for m in range(2)]

    def scores(j, s_buf, c):
        kb = k_ref[pl.ds(pl.multiple_of(j * tk, tk), tk), :]
        s_buf[c] = jnp.dot(kb, qz_scr[c], preferred_element_type=F32)

    def softmax(s_buf, p_buf, al_buf, masked, c):
        s = s_buf[c]
        if masked is not None:
            keyi = lax.broadcasted_iota(jnp.int32, s.shape, 0) + masked
            qryi = lax.broadcasted_iota(jnp.int32, s.shape, 1) + c[1] * A_QCHUNK
            s = jnp.where(keyi <= qryi, s, -jnp.inf)
        m_old = m_scr[c]
        m_new = jnp.maximum(m_old, jnp.max(s, axis=0, keepdims=True))
        p_buf[c] = jnp.exp2(s - m_new).astype(BF16)
        al_buf[c] = jnp.exp2(m_old - m_new)
        m_scr[c] = m_new

    def values(j, p_buf, al_buf, c):
        vtb = vt_ref[j]
        acc_scr[c] = al_buf[c] * acc_scr[c] + jnp.dot(vtb, p_buf[c], preferred_element_type=F32)

    def stage(val=None, sco=None, sm=None):
        for c in chunks:
            if sco is not None:
                scores(*sco, c)
            if val is not None:
                values(*val, c)
            if sm is not None:
                softmax(*sm, c)

    A = (p_a, al_a)
    B = (p_b, al_b)

    assert tq == 2 * tk
    last = 2 * i + 1
    stage(sco=(0, s_a))

    @pl.when(i == 0)
    def _():
        stage(sco=(1, s_b), sm=(s_a, *A, 0))

    @pl.when(i >= 1)
    def _():
        stage(sco=(1, s_b), sm=(s_a, *A, None))

    def pair(t):
        stage(sco=(t, s_a), val=(t - 2, *A), sm=(s_b, *B, None))
        stage(sco=(t + 1, s_b), val=(t - 1, *B), sm=(s_a, *A, None))

    def four_pairs(u, carry):
        for k in range(4):
            pair(2 + 8 * u + 2 * k)
        return carry

    n_pairs = jnp.maximum(i - 1, 0)
    n_quads = n_pairs // 4
    lax.fori_loop(0, n_quads, four_pairs, 0)

    def one_pair(u, carry):
        pair(2 + 8 * n_quads + 2 * u)
        return carry

    lax.fori_loop(0, n_pairs % 4, one_pair, 0)

    @pl.when(i >= 1)
    def _():
        stage(sco=(last - 1, s_a), val=(last - 3, *A), sm=(s_b, *B, None))
        stage(sco=(last, s_b), val=(last - 2, *B), sm=(s_a, *A, 0))

    stage(val=(last - 1, *A))
    stage(sm=(s_b, *B, tk))
    stage(val=(last, *B))

    lam = (jnp.exp(jnp.sum(lq1_ref[...] * lk1_ref[...], axis=-1, keepdims=True))
           - jnp.exp(jnp.sum(lq2_ref[...] * lk2_ref[...], axis=-1, keepdims=True)) + lam_init)
    dv = 2 * A_HEAD_DIM
    acc = [jnp.concatenate([acc_scr[m, h] for h in range(n_half)], axis=1) for m in range(2)]
    o0 = acc[0][0:dv] * (1.0 / acc[0][dv:dv + 1])
    o1 = acc[1][0:dv] * (1.0 / acc[1][dv:dv + 1])
    o = o0 - lam * o1
    y = o * lax.rsqrt(jnp.mean(o * o, axis=0, keepdims=True) + EPS) * g_ref[...]
    y = (y * (1.0 - lam_init)).T
    o_ref[...] = (y * za_ref[...].astype(F32)).astype(o_ref.dtype)


def _attn_a(qt, ka, vt, za, subln_g, lq1, lk1, lq2, lk2, lam_init, tq=2 * ATTN_A_BLOCK):
    seq = ka.shape[0]
    nk, tk = vt.shape[1], vt.shape[3]
    chunked = lambda rows, dtype: pltpu.VMEM((2, tq // A_QCHUNK, rows, A_QCHUNK), dtype)
    fixed = lambda h, i: (0, 0)
    vec = lambda a: a.reshape(1, -1)
    return pl.pallas_call(
        functools.partial(_attn_a_kernel, tq=tq, tk=tk, lam_init=lam_init),
        grid=(A_HEADS, seq // tq),
        in_specs=[pl.BlockSpec((LANES, tq), lambda h, i: (h, i)),
                  pl.BlockSpec((seq, LANES), lambda h, i: (0, h)),
                  pl.BlockSpec((None, nk, LANES + A_ONES_ROWS, tk), lambda h, i: (h, 0, 0, 0)),
                  pl.BlockSpec((tq, LANES), lambda h, i: (i, h)),
                  pl.BlockSpec((2 * A_HEAD_DIM, 1), fixed),
                  pl.BlockSpec((1, A_HEAD_DIM), fixed),
                  pl.BlockSpec((1, A_HEAD_DIM), fixed),
                  pl.BlockSpec((1, A_HEAD_DIM), fixed),
                  pl.BlockSpec((1, A_HEAD_DIM), fixed)],
        out_specs=pl.BlockSpec((tq, LANES), lambda h, i: (i, h)),
        out_shape=jax.ShapeDtypeStruct((seq, A_WIDTH), BF16),
        scratch_shapes=[chunked(LANES, BF16),
                        chunked(1, F32),
                        chunked(2 * A_HEAD_DIM + A_ONES_ROWS, F32),
                        chunked(tk, F32), chunked(tk, F32),
                        chunked(tk, BF16), chunked(tk, BF16),
                        chunked(1, F32), chunked(1, F32)],
        compiler_params=pltpu.CompilerParams(dimension_semantics=("arbitrary", "arbitrary"),
                                             vmem_limit_bytes=VMEM_LIMIT),
        name="attn_a",
    )(qt, ka, vt, za, subln_g.reshape(-1, 1), vec(lq1), vec(lk1), vec(lq2), vec(lk2))


C_TILE = C_BLOCK * max(dil for _, dil in C_PATTERNS)


def _attn_c_kernel(q_ref, kp_ref, kc_ref, vp_ref, vc_ref, z_ref, bias_ref, o_ref, out_scr, lse_scr):
    first_tile = (pl.program_id(1) == 0).astype(jnp.int32)
    head0 = lax.broadcasted_iota(jnp.int32, (LANES, C_BLOCK), 0) < C_HEAD_DIM

    def pick(t):
        return jnp.where(head0, t[:, :C_BLOCK], t[:, C_BLOCK:])

    def strided(start, dil):
        return pl.ds(start, C_BLOCK) if dil == 1 else pl.ds(start, C_BLOCK, stride=dil)

    for p, (window, dil) in enumerate(C_PATTERNS):
        assert window // dil == C_BLOCK
        span = C_BLOCK * dil
        n_blocks = C_TILE // C_BLOCK
        idx = []
        for b in range(n_blocks):
            u, r = divmod(b, dil)
            older = strided(C_TILE - span + r, dil) if u == 0 else strided((u - 1) * span + r, dil)
            idx.append((strided(u * span + r, dil), older, u == 0))

        def keys(prev_ref, cur_ref, q_rows, older, in_prev):
            return jnp.concatenate([(prev_ref if in_prev else cur_ref)[older, :], cur_ref[q_rows, :]], axis=0)

        q_both, v_t, s, e, stats, o_all = [], [], [], [], [], []
        for q_rows, older, in_prev in idx:
            qt = q_ref[q_rows, :].T
            zero = jnp.zeros_like(qt)
            q_both.append(jnp.concatenate([jnp.where(head0, qt, zero), jnp.where(head0, zero, qt)],
                                          axis=1).astype(BF16))
            v_t.append(keys(vp_ref, vc_ref, q_rows, older, in_prev).T.astype(BF16))
        for k, (q_rows, older, in_prev) in enumerate(idx):
            k2 = keys(kp_ref, kc_ref, q_rows, older, in_prev).astype(BF16)
            bias = bias_ref[first_tile] if in_prev else bias_ref[0]
            s.append(jnp.dot(k2, q_both[k], preferred_element_type=F32) + bias)
        for k in range(n_blocks):
            mx = jnp.max(s[k], axis=0, keepdims=True)
            ek = jnp.exp2(s[k] - mx)
            den = jnp.sum(ek, axis=0, keepdims=True)
            e.append(ek.astype(BF16))
            stats.append((1.0 / den, mx + jnp.log2(den)))
        for k in range(n_blocks):
            o_all.append(jnp.dot(v_t[k], e[k], preferred_element_type=F32))
        wide = (LANES, 2 * C_BLOCK)
        for k, (q_rows, _, _) in enumerate(idx):
            inv, lse = stats[k]
            o_t = pick(o_all[k]) * pick(jnp.broadcast_to(inv, wide))
            l_t = pick(jnp.broadcast_to(lse, wide))
            out_scr[p, q_rows, :] = o_t.T
            lse_scr[p, q_rows, :] = l_t.T

    l0, l1, l2 = lse_scr[0], lse_scr[1], lse_scr[2]
    mx = jnp.maximum(jnp.maximum(l0, l1), l2)
    w0 = jnp.exp2(l0 - mx)
    w1 = jnp.exp2(l1 - mx)
    w2 = jnp.exp2(l2 - mx)
    oc = (w0 * out_scr[0] + w1 * out_scr[1] + w2 * out_scr[2]) / (w0 + w1 + w2)
    o_ref[...] = (oc * z_ref[...].astype(F32)).astype(o_ref.dtype)


def _attn_c_bias():
    ki = np.arange(2 * C_BLOCK)[:, None]
    qi = np.tile(np.arange(C_BLOCK), LANES // C_HEAD_DIM)[None, :]
    rel = qi + C_BLOCK - ki
    band = (rel >= 0) & (rel <= C_BLOCK)
    masks = np.stack([band, band & (ki >= C_BLOCK)])
    return jnp.asarray(np.where(masks, 0.0, -np.inf), dtype=F32)


def _attn_c(qc, kc, vc, zc):
    seq = qc.shape[0]
    cur = lambda c, t: (t, c)
    prev = lambda c, t: (jnp.maximum(t - 1, 0), c)
    tile = (C_TILE, LANES)
    return pl.pallas_call(
        _attn_c_kernel,
        grid=(C_WIDTH // LANES, seq // C_TILE),
        in_specs=[pl.BlockSpec(tile, cur),
                  pl.BlockSpec(tile, prev), pl.BlockSpec(tile, cur),
                  pl.BlockSpec(tile, prev), pl.BlockSpec(tile, cur),
                  pl.BlockSpec(tile, cur),
                  pl.BlockSpec((2, 2 * C_BLOCK, 2 * C_BLOCK), lambda c, t: (0, 0, 0))],
        out_specs=pl.BlockSpec(tile, cur),
        out_shape=jax.ShapeDtypeStruct((seq, C_WIDTH), BF16),
        scratch_shapes=[pltpu.VMEM((len(C_PATTERNS), C_TILE, LANES), F32),
                        pltpu.VMEM((len(C_PATTERNS), C_TILE, LANES), F32)],
        compiler_params=pltpu.CompilerParams(dimension_semantics=("arbitrary", "arbitrary"),
                                             vmem_limit_bytes=VMEM_LIMIT),
        name="attn_c",
    )(qc, kc, kc, vc, vc, zc, _attn_c_bias())


def _merge_kernel(x_ref, ya_ref, yc_ref, guv_ref, zb_ref, ng_ref, wg_ref, lng_ref, lnb_ref,
                  sw_ref, sb_ref, wb_ref, wo_ref, fg_ref, o_ref, *, final_norm):
    tm = x_ref.shape[0]
    guv = guv_ref[...]
    u = guv[:, :B_WIDTH].astype(F32)
    vb = guv[:, B_WIDTH:].astype(F32)
    mu = jnp.mean(vb, axis=-1, keepdims=True)
    var = jnp.mean(jnp.square(vb - mu), axis=-1, keepdims=True)
    vb = ((vb - mu) * lax.rsqrt(var + EPS) * lng_ref[...] + lnb_ref[...]).astype(BF16)
    ti = lax.broadcasted_iota(jnp.int32, (B_CHUNK, B_CHUNK), 0)
    si = lax.broadcasted_iota(jnp.int32, (B_CHUNK, B_CHUNK), 1)
    causal = si <= ti
    rows = []
    for c in range(tm // B_CHUNK):
        cols = []
        for g in range(B_GROUPS):
            w = jnp.where(causal, sw_ref[g], 0.0).astype(BF16)
            blk = vb[c * B_CHUNK:(c + 1) * B_CHUNK, g * B_GROUP_DIM:(g + 1) * B_GROUP_DIM]
            mixed = jnp.dot(w, blk, preferred_element_type=F32) + sb_ref[:, g:g + 1]
            cols.append(mixed)
        rows.append(jnp.concatenate(cols, axis=-1))
    mixed = jnp.concatenate(rows, axis=0)
    yb = (u * mixed * zb_ref[...].astype(F32)).astype(BF16)

    x = x_ref[...]
    hn = (x * lax.rsqrt(jnp.mean(x * x, axis=-1, keepdims=True) + EPS) * ng_ref[...]).astype(BF16)
    merged = jnp.zeros((tm, D_MODEL), F32)
    for n, y in enumerate((ya_ref[...], yb, yc_ref[...])):
        pb = jnp.dot(y, wb_ref[n], preferred_element_type=F32)
        gate = _sigmoid(jnp.dot(hn, wg_ref[:, n * D_MODEL:(n + 1) * D_MODEL], preferred_element_type=F32))
        merged = merged + gate * pb
    out = x + jnp.dot(merged.astype(BF16), wo_ref[...], preferred_element_type=F32)
    if final_norm:
        out = out * lax.rsqrt(jnp.mean(out * out, axis=-1, keepdims=True) + EPS) * fg_ref[...]
    o_ref[...] = out


def _merge(x2, ya, yc, guv, zb, norm_g, wg_bf16, ln_g, ln_b, sgu_w, sgu_b, wb_bf16, wo_bf16, layer,
           final_g, final_norm, tm=MERGE_ROWS):
    seq = x2.shape[0]
    row = lambda i: (i, 0)
    fixed2 = lambda i: (0, 0)
    fixed3 = lambda i: (0, 0, 0)
    return pl.pallas_call(
        functools.partial(_merge_kernel, final_norm=final_norm),
        grid=(seq // tm,),
        in_specs=[pl.BlockSpec((tm, D_MODEL), row),
                  pl.BlockSpec((tm, A_WIDTH), row),
                  pl.BlockSpec((tm, C_WIDTH), row),
                  pl.BlockSpec((tm, 2 * B_WIDTH), row),
                  pl.BlockSpec((tm, B_WIDTH), row),
                  pl.BlockSpec((1, D_MODEL), fixed2),
                  pl.BlockSpec((None, D_MODEL, N_BRANCH * D_MODEL), lambda i: (layer, 0, 0)),
                  pl.BlockSpec((1, B_WIDTH), fixed2),
                  pl.BlockSpec((1, B_WIDTH), fixed2),
                  pl.BlockSpec((B_GROUPS, B_CHUNK, B_CHUNK), fixed3),
                  pl.BlockSpec((B_CHUNK, B_GROUPS), fixed2),
                  pl.BlockSpec((None, N_BRANCH, BRANCH_WIDTH, D_MODEL), lambda i: (layer, 0, 0, 0)),
                  pl.BlockSpec((None, D_MODEL, D_MODEL), lambda i: (layer, 0, 0)),
                  pl.BlockSpec((1, D_MODEL), fixed2)],
        out_specs=pl.BlockSpec((tm, D_MODEL), row),
        out_shape=jax.ShapeDtypeStruct((seq, D_MODEL), F32),
        compiler_params=pltpu.CompilerParams(dimension_semantics=("arbitrary",),
                                             vmem_limit_bytes=VMEM_LIMIT),
        name="merge",
    )(x2, ya, yc, guv, zb, norm_g.reshape(1, -1), wg_bf16, ln_g.reshape(1, -1), ln_b.reshape(1, -1), sgu_w, sgu_b.T,
      wb_bf16, wo_bf16, final_g.reshape(1, -1))


def kernel(x, positions, norm_g, w_in, lam_q1, lam_k1, lam_q2, lam_k2, subln_g, sgu_ln_g, sgu_ln_b,
           sgu_w, sgu_b, w_branch, w_out, final_g):
    batch, seq, _ = x.shape
    depth = norm_g.shape[0]
    assert batch == 1 and seq % C_TILE == 0
    h = x.reshape(seq, D_MODEL)
    cos_t, sin_t = _rope_tables(positions.reshape(seq), seq)
    w_in, w_branch, w_out = w_in.astype(BF16), w_branch.astype(BF16), w_out.astype(BF16)
    w_gate = w_in[:, :, _OFFS[10]:]
    for l in range(depth):
        lam_init = 0.8 - 0.6 * math.exp(-0.3 * l)
        (qt, ka, vt, za, guv, zb, qc, kc, vc, zc) = _inproj(
            h, norm_g[l], w_in, l, cos_t, sin_t)
        ya = _attn_a(qt, ka, vt, za, subln_g[l], lam_q1[l], lam_k1[l], lam_q2[l], lam_k2[l], lam_init)
        yc = _attn_c(qc, kc, vc, zc)
        h = _merge(h, ya, yc, guv, zb, norm_g[l], w_gate, sgu_ln_g[l], sgu_ln_b[l], sgu_w[l], sgu_b[l],
                   w_branch, w_out, l, final_g,
                   final_norm=(l == depth - 1))
    return h.reshape(batch, seq, D_MODEL)
```
